```python
import math
import jax, jax.numpy as jnp
from jax import lax
import numpy as np

D_MODEL = 1024
BATCH = 4
SEQ = 8192
DEPTH = 2
DEC_BATCH = 32
DEC_SEQ = 16
PAST_LEN = 4096

CHUNK = 64
N_MIXERS = 2
N_HGRN_LAYERS = (DEPTH + 1) // 2
N_GLA_LAYERS = DEPTH // 2
HGRN_EXPAND = 128
HGRN_HEADS = D_MODEL // HGRN_EXPAND
HGRN_DK = HGRN_EXPAND
HGRN_DV = D_MODEL // HGRN_HEADS
HGRN_FDIM = HGRN_HEADS * HGRN_DK
GLA_HEADS = 4
GLA_KEY_DIM = D_MODEL // 2
GLA_VAL_DIM = D_MODEL
GLA_DK = GLA_KEY_DIM // GLA_HEADS
GLA_DV = GLA_VAL_DIM // GLA_HEADS
GLA_GATE_RANK = 16
GLA_GATE_NORMALIZER = 16.0
D_FF = 2816
CONV_WIDTH = 3
ALPHA = (2.0 * DEPTH) ** 0.25
BETA = (8.0 * DEPTH) ** -0.25
LN_EPS = 1e-5
RMS_EPS = 1e-6

kernel_name = "hgrn2_gla_convffn_streaming_step"


def layer_norm(x, g, b):
    xf = x.astype(jnp.float32)
    mu = jnp.mean(xf, axis=-1, keepdims=True)
    xc = xf - mu
    var = jnp.mean(xc * xc, axis=-1, keepdims=True)
    return (xc * lax.rsqrt(var + LN_EPS) * g.astype(jnp.float32) + b.astype(jnp.float32)).astype(x.dtype)


def rms_norm(x, g):
    xf = x.astype(jnp.float32)
    ms = jnp.mean(xf * xf, axis=-1, keepdims=True)
    return xf * lax.rsqrt(ms + RMS_EPS) * g.astype(jnp.float32)


def chunked_gated_linear_attention(q, k, v, logf, s0, chunk):
    B, T, H, DK = q.shape
    DV = v.shape[-1]
    n = T // chunk

    def blocks(a):
        return a.astype(jnp.float32).reshape(B, n, chunk, H, a.shape[-1]).transpose(1, 0, 3, 2, 4)

    causal = jnp.tril(jnp.ones((chunk, chunk), dtype=bool))[:, :, None]

    def step(S, inp):
        qc, kc, vc, gc = inp
        b = jnp.cumsum(gc, axis=2)
        diff = b[:, :, :, None, :] - b[:, :, None, :, :]
        decay = jnp.exp(jnp.where(causal, diff, -jnp.inf))
        scores = jnp.einsum('bhtd,bhsd,bhtsd->bhts', qc, kc, decay)
        o = (jnp.einsum('bhts,bhsv->bhtv', scores, vc)
             + jnp.einsum('bhtd,bhdv->bhtv', qc * jnp.exp(b), S))
        b_last = b[:, :, -1, :]
        S_new = (jnp.exp(b_last)[..., None] * S
                 + jnp.einsum('bhsd,bhsv->bhdv', kc * jnp.exp(b_last[:, :, None, :] - b), vc))
        return S_new, o

    S_fin, o = lax.scan(step, s0.astype(jnp.float32), (blocks(q), blocks(k), blocks(v), blocks(logf)))
    o = o.transpose(1, 0, 3, 2, 4).reshape(B, T, H, DV)
    return o, S_fin


def hgrn2_mixer(x, s0, lb, w_in, w_out, norm_g, chunk):
    B, T, _ = x.shape
    proj = x @ w_in
    q, fz, i, g = jnp.split(proj, [HGRN_FDIM, 2 * HGRN_FDIM, 3 * HGRN_FDIM], axis=-1)
    q = jax.nn.silu(q.astype(jnp.float32)).reshape(B, T, HGRN_HEADS, HGRN_DK)
    lbh = lb.reshape(HGRN_HEADS, HGRN_DK)
    f = lbh + (1.0 - lbh) * jax.nn.sigmoid(fz.astype(jnp.float32).reshape(B, T, HGRN_HEADS, HGRN_DK))
    logf = jnp.log(f)
    k = 1.0 - f
    v = i.reshape(B, T, HGRN_HEADS, HGRN_DV)
    o, s_new = chunked_gated_linear_attention(q, k, v, logf, s0, chunk)
    gate = jax.nn.silu(g.astype(jnp.float32)).reshape(B, T, HGRN_HEADS, HGRN_DV)
    o = (rms_norm(o, norm_g) * gate).astype(x.dtype).reshape(B, T, D_MODEL)
    return o @ w_out, s_new.astype(s0.dtype)


def gla_mixer(x, s0, w_in, w_gk2, b_gk2, w_out, norm_g, chunk):
    B, T, _ = x.shape
    proj = x @ w_in
    s1 = GLA_KEY_DIM
    s2 = 2 * GLA_KEY_DIM
    s3 = s2 + GLA_VAL_DIM
    s4 = s3 + GLA_VAL_DIM
    q, k, v, g, r = jnp.split(proj, [s1, s2, s3, s4], axis=-1)
    q = q.reshape(B, T, GLA_HEADS, GLA_DK) * (GLA_DK ** -0.5)
    k = k.reshape(B, T, GLA_HEADS, GLA_DK)
    v = v.reshape(B, T, GLA_HEADS, GLA_DV)
    gk = jax.nn.log_sigmoid((r @ w_gk2 + b_gk2).astype(jnp.float32)) / GLA_GATE_NORMALIZER
    gk = gk.reshape(B, T, GLA_HEADS, GLA_DK)
    o, s_new = chunked_gated_linear_attention(q, k, v, gk, s0, chunk)
    gate = jax.nn.silu(g.astype(jnp.float32)).reshape(B, T, GLA_HEADS, GLA_DV)
    o = (rms_norm(o, norm_g) * gate).astype(x.dtype).reshape(B, T, GLA_VAL_DIM)
    return o @ w_out, s_new.astype(s0.dtype)


def conv_ffn(x, prev, w_up, conv_w, conv_b, w_down):
    L = x.shape[1]
    a, b = jnp.split(x @ w_up, 2, axis=-1)
    a_full = jnp.concatenate([prev.astype(a.dtype), a], axis=1)
    conv = conv_b
    for j in range(CONV_WIDTH):
        conv = conv + a_full[:, j:j + L] * conv_w[j]
    h = jax.nn.gelu(conv) * b
    return h @ w_down, a_full[:, -(CONV_WIDTH - 1):]


def run_trunk(x, st_hgrn, st_gla, st_conv, chunk, lb_all,
              hgrn_w_in, hgrn_w_out, hgrn_norm_g,
              gla_w_in, gla_w_gk2, gla_b_gk2, gla_w_out, gla_norm_g,
              ln_mix_g, ln_mix_b, ffn_w_up, ffn_conv_w, ffn_conv_b, ffn_w_down,
              ln_ffn_g, ln_ffn_b):
    new_h, new_g, new_c = [], [], []
    for i in range(DEPTH):
        j = i // N_MIXERS
        if i % N_MIXERS == 0:
            m, s = hgrn2_mixer(x, st_hgrn[j], lb_all[i], hgrn_w_in[j], hgrn_w_out[j], hgrn_norm_g[j], chunk)
            new_h.append(s)
        else:
            m, s = gla_mixer(x, st_gla[j], gla_w_in[j], gla_w_gk2[j], gla_b_gk2[j], gla_w_out[j], gla_norm_g[j], chunk)
            new_g.append(s)
        x = layer_norm(ALPHA * x + m, ln_mix_g[i], ln_mix_b[i])
        f, c = conv_ffn(x, st_conv[i], ffn_w_up[i], ffn_conv_w[i], ffn_conv_b[i], ffn_w_down[i])
        new_c.append(c)
        x = layer_norm(ALPHA * x + f, ln_ffn_g[i], ln_ffn_b[i])
    return x, jnp.stack(new_h), jnp.stack(new_g), jnp.stack(new_c)


def setup_inputs(seed: int = 0) -> dict:
    key = jax.random.key(seed)
    ks = jax.random.split(key, 24)
    f32 = jnp.float32
    D = D_MODEL
    hgrn_cols = 3 * HGRN_FDIM + D
    hgrn_scale = jnp.concatenate([jnp.ones((2 * HGRN_FDIM,), f32), jnp.full((D,), BETA, f32), jnp.ones((D,), f32)])
    gla_cols = 2 * GLA_KEY_DIM + 2 * GLA_VAL_DIM + GLA_GATE_RANK
    gla_scale = jnp.concatenate([jnp.ones((2 * GLA_KEY_DIM,), f32), jnp.full((GLA_VAL_DIM,), BETA, f32),
                                 jnp.ones((GLA_VAL_DIM + GLA_GATE_RANK,), f32)])
    inputs = {
        "x_prompt": jax.random.normal(ks[0], (BATCH, SEQ, D), f32),
        "x_sample": jax.random.normal(ks[1], (DEC_BATCH, DEC_SEQ, D), f32),
        "state_hgrn": 0.5 * jax.random.normal(ks[2], (N_HGRN_LAYERS, DEC_BATCH, HGRN_HEADS, HGRN_DK, HGRN_DV), f32),
        "state_gla": 0.5 * jax.random.normal(ks[3], (N_GLA_LAYERS, DEC_BATCH, GLA_HEADS, GLA_DK, GLA_DV), f32),
        "state_ffn_conv": jax.random.normal(ks[4], (DEPTH, DEC_BATCH, CONV_WIDTH - 1, D_FF), f32),
        "lb_param": 0.1 * jax.random.normal(ks[5], (DEPTH + 1, HGRN_FDIM), f32),
        "hgrn_w_in": jax.random.normal(ks[6], (N_HGRN_LAYERS, D, hgrn_cols), f32) * (D ** -0.5) * hgrn_scale,
        "hgrn_w_out": jax.random.normal(ks[7], (N_HGRN_LAYERS, D, D), f32) * (D ** -0.5) * BETA,
        "hgrn_norm_g": 1.0 + 0.01 * jax.random.normal(ks[8], (N_HGRN_LAYERS, HGRN_DV), f32),
        "gla_w_in": jax.random.normal(ks[9], (N_GLA_LAYERS, D, gla_cols), f32) * (D ** -0.5) * gla_scale,
        "gla_w_gk2": jax.random.normal(ks[10], (N_GLA_LAYERS, GLA_GATE_RANK, GLA_KEY_DIM), f32) * (GLA_GATE_RANK ** -0.5),
        "gla_b_gk2": 0.01 * jax.random.normal(ks[11], (N_GLA_LAYERS, GLA_KEY_DIM), f32),
        "gla_w_out": jax.random.normal(ks[12], (N_GLA_LAYERS, GLA_VAL_DIM, D), f32) * (GLA_VAL_DIM ** -0.5) * BETA,
        "gla_norm_g": 1.0 + 0.01 * jax.random.normal(ks[13], (N_GLA_LAYERS, GLA_DV), f32),
        "ln_mix_g": 1.0 + 0.01 * jax.random.normal(ks[14], (DEPTH, D), f32),
        "ln_mix_b": 0.01 * jax.random.normal(ks[15], (DEPTH, D), f32),
        "ffn_w_up": jax.random.normal(ks[16], (DEPTH, D, 2 * D_FF), f32) * (D ** -0.5),
        "ffn_conv_w": jax.random.normal(ks[17], (DEPTH, CONV_WIDTH, D_FF), f32) * (CONV_WIDTH ** -0.5),
        "ffn_conv_b": 0.01 * jax.random.normal(ks[18], (DEPTH, D_FF), f32),
        "ffn_w_down": jax.random.normal(ks[19], (DEPTH, D_FF, D), f32) * (D_FF ** -0.5) * BETA,
        "ln_ffn_g": 1.0 + 0.01 * jax.random.normal(ks[20], (DEPTH, D), f32),
        "ln_ffn_b": 0.01 * jax.random.normal(ks[21], (DEPTH, D), f32),
    }
    return inputs


def reference(x_prompt, x_sample, state_hgrn, state_gla, state_ffn_conv, lb_param,
              hgrn_w_in, hgrn_w_out, hgrn_norm_g,
              gla_w_in, gla_w_gk2, gla_b_gk2, gla_w_out, gla_norm_g,
              ln_mix_g, ln_mix_b, ffn_w_up, ffn_conv_w, ffn_conv_b, ffn_w_down,
              ln_ffn_g, ln_ffn_b):
    lb_all = jnp.cumsum(jax.nn.softmax(lb_param.astype(jnp.float32), axis=0), axis=0)
    weights = (hgrn_w_in, hgrn_w_out, hgrn_norm_g, gla_w_in, gla_w_gk2, gla_b_gk2, gla_w_out, gla_norm_g,
               ln_mix_g, ln_mix_b, ffn_w_up, ffn_conv_w, ffn_conv_b, ffn_w_down, ln_ffn_g, ln_ffn_b)
    dt = x_prompt.dtype
    z_hgrn = jnp.zeros((N_HGRN_LAYERS, BATCH, HGRN_HEADS, HGRN_DK, HGRN_DV), dt)
    z_gla = jnp.zeros((N_GLA_LAYERS, BATCH, GLA_HEADS, GLA_DK, GLA_DV), dt)
    z_conv = jnp.zeros((DEPTH, BATCH, CONV_WIDTH - 1, D_FF), dt)
    y_prompt, hgrn_p, gla_p, conv_p = run_trunk(x_prompt, z_hgrn, z_gla, z_conv, CHUNK, lb_all, *weights)
    y_sample, hgrn_s, gla_s, conv_s = run_trunk(x_sample, state_hgrn, state_gla, state_ffn_conv,
                                                x_sample.shape[1], lb_all, *weights)
    return (y_prompt, y_sample, hgrn_p, hgrn_s, gla_p, gla_s, conv_p, conv_s)
```

```python
import functools
import math

import jax
import jax.numpy as jnp
from jax import lax
from jax.experimental import pallas as pl
from jax.experimental.pallas import tpu as pltpu

F32 = jnp.float32
BF16 = jnp.bfloat16

D_MODEL = 1024
DEPTH = 2
CHUNK = 64
HGRN_HEADS = 8
HGRN_DK = 128
HGRN_DV = 128
HGRN_FDIM = HGRN_HEADS * HGRN_DK
GLA_HEADS = 4
GLA_KEY_DIM = 512
GLA_VAL_DIM = 1024
GLA_DK = 128
GLA_DV = 256
GLA_GATE_RANK = 16
GLA_GATE_NORMALIZER = 16.0
D_FF = 2816
CONV_WIDTH = 3
ALPHA = (2.0 * DEPTH) ** 0.25
LN_EPS = 1e-5
RMS_EPS = 1e-6

SUBLANES = 8
LANES = 128
VMEM_LIMIT_BYTES = 56 * 1024 * 1024
PROMPT_ROWS = 512
FFN_COLS = 256


def _dot(a, b):
    return jnp.dot(a, b, preferred_element_type=F32)


def _dot_nt(a, b):
    return lax.dot_general(a, b, (((1,), (1,)), ((), ())), preferred_element_type=F32)


def _dot_tn(a, b):
    return lax.dot_general(a, b, (((0,), (0,)), ((), ())), preferred_element_type=F32)


def _sigmoid(x):
    return 1.0 / (1.0 + jnp.exp(-x))


def _silu(x):
    return x * _sigmoid(x)


def _gelu_tanh(x):
    c = math.sqrt(2.0 / math.pi)
    return x * (0.5 * (1.0 + jnp.tanh(c * (x + 0.044715 * (x * x * x)))))


def _log_sigmoid(x):
    return jnp.minimum(x, 0.0) - jnp.log(1.0 + jnp.exp(-jnp.abs(x)))


def _layer_norm(x, g, b):
    mu = jnp.mean(x, axis=-1, keepdims=True)
    xc = x - mu
    var = jnp.mean(xc * xc, axis=-1, keepdims=True)
    return xc * lax.rsqrt(var + LN_EPS) * g + b


def _cumsum_rows(g):
    c = g.shape[0]
    r = lax.broadcasted_iota(jnp.int32, (c, c), 0)
    s = lax.broadcasted_iota(jnp.int32, (c, c), 1)
    tri = (s <= r).astype(BF16)
    hi = g.astype(BF16)
    r1 = g - hi.astype(F32)
    mid = r1.astype(BF16)
    lo = (r1 - mid.astype(F32)).astype(BF16)
    return _dot(tri, hi) + _dot(tri, mid) + _dot(tri, lo)


def _head_attention(q_ref, k_ref, b_ref, cols, v, st, chunk):
    nb = chunk // SUBLANES
    dk = cols.stop - cols.start

    def rows(ref, i):
        return ref[SUBLANES * i:SUBLANES * (i + 1), cols]

    def row(ref, t):
        return jnp.broadcast_to(ref[t:t + 1, cols], (SUBLANES, dk))

    q_all = q_ref[:, cols]
    k_all = k_ref[:, cols]
    b_all = b_ref[:, cols]
    b_last = b_ref[chunk - 1:chunk, cols]

    kpp = jnp.concatenate(
        [rows(k_ref, j) * jnp.exp(row(b_ref, SUBLANES * j + SUBLANES - 1) - rows(b_ref, j))
         for j in range(nb)], axis=0).astype(BF16)

    col = lax.broadcasted_iota(jnp.int32, (SUBLANES, chunk), 1)
    rw = lax.broadcasted_iota(jnp.int32, (SUBLANES, chunk), 0)

    if nb > 1:
        lq = []
        for i in range(1, nb):
            qi, bi = rows(q_ref, i), rows(b_ref, i)
            for j in range(i):
                lq.append(qi * jnp.exp(bi - row(b_ref, SUBLANES * j + SUBLANES - 1)))
        cross = _dot_nt(jnp.concatenate(lq, axis=0).astype(BF16), kpp)

    a_rows = []
    p = 0
    for i in range(nb):
        acc = jnp.zeros((SUBLANES, chunk), F32)
        for j in range(i):
            blk = cross[SUBLANES * p:SUBLANES * (p + 1), :]
            acc = jnp.where((col >= SUBLANES * j) & (col < SUBLANES * (j + 1)), blk, acc)
            p += 1
        qi, bi = rows(q_ref, i), rows(b_ref, i)
        for s in range(SUBLANES):
            t = SUBLANES * i + s
            e = jnp.exp(jnp.minimum(bi - row(b_ref, t), 0.0))
            cs = jnp.sum(qi * e * row(k_ref, t), axis=-1, keepdims=True)
            acc = jnp.where(col == t, cs, acc)
        acc = jnp.where(col <= SUBLANES * i + rw, acc, 0.0)
        a_rows.append(acc)
    a = jnp.concatenate(a_rows, axis=0).astype(BF16)

    vb = v.astype(BF16)
    qe = (q_all * jnp.exp(b_all)).astype(BF16)
    o = _dot(a, vb) + _dot_nt(qe, st.astype(BF16))
    kd = (k_all * jnp.exp(b_last - b_all)).astype(BF16)
    st_new = jnp.exp(b_last) * st + _dot_tn(vb, kd)
    return o, st_new


def _rms_gate(o, norm_g, gate):
    ms = jnp.mean(o * o, axis=-1, keepdims=True)
    return o * lax.rsqrt(ms + RMS_EPS) * norm_g * _silu(gate)


def _mixer_kernel(*refs, kind, layer, bb, nch, chunk, has_s0):
    it = iter(refs)
    x_ref = next(it)
    s0_ref = next(it) if has_s0 else None
    if kind == "hgrn":
        lb_ref = next(it)
        w_in_ref = next(it)
    else:
        w_in_ref = next(it)
        w_r_ref = next(it)
        w_gk2_ref = next(it)
        b_gk2_ref = next(it)
    w_out_ref = next(it)
    ng_ref = next(it)
    lng_ref = next(it)
    lnb_ref = next(it)
    y_ref = next(it)
    so_ref = next(it)
    proj_ref = next(it)
    gk_ref = next(it) if kind == "gla" else None
    q_ref = next(it)
    k_ref = next(it)
    b_ref = next(it)
    o_ref = next(it)
    st_ref = next(it)

    heads, dk, dv = (HGRN_HEADS, HGRN_DK, HGRN_DV) if kind == "hgrn" else (GLA_HEADS, GLA_DK, GLA_DV)
    kdim = heads * dk
    rows = bb * nch * chunk
    t = pl.program_id(1)
    nt = pl.num_programs(1)

    @pl.when(t == 0)
    def _init_state():
        for b in range(bb):
            for h in range(heads):
                if has_s0:
                    st_ref[b, h] = s0_ref[b, h].T
                else:
                    st_ref[b, h] = jnp.zeros((dv, dk), F32)

    x = x_ref[...].reshape(rows, D_MODEL)
    xb = x.astype(BF16)
    ncols = proj_ref.shape[1]
    step = 512
    for n0 in range(0, ncols, step):
        proj_ref[:, n0:n0 + step] = _dot(xb, w_in_ref[:, n0:n0 + step])

    if kind == "hgrn":
        p = lb_ref[...]
        pe = jnp.exp(p - jnp.max(p, axis=0, keepdims=True))
        lb = jnp.sum(pe[:layer + 1], axis=0, keepdims=True) / jnp.sum(pe, axis=0, keepdims=True)
        q_off, f_off, v_off, g_off = 0, kdim, 2 * kdim, 3 * kdim
    else:
        r = _dot(xb, w_r_ref[...]).astype(BF16)
        gk_ref[...] = _log_sigmoid(_dot(r, w_gk2_ref[...]) + b_gk2_ref[...]) * (1.0 / GLA_GATE_NORMALIZER)
        q_off, k_off, v_off, g_off = 0, kdim, 2 * kdim, 2 * kdim + heads * dv

    ng = ng_ref[...]

    def chunk_body(ci, carry):
        r0 = pl.multiple_of(ci * chunk, chunk)
        rs = pl.ds(r0, chunk)
        b_idx = ci // nch
        if kind == "hgrn":
            q_ref[...] = _silu(proj_ref[rs, q_off:q_off + kdim])
            f = lb + (1.0 - lb) * _sigmoid(proj_ref[rs, f_off:f_off + kdim])
            k_ref[...] = 1.0 - f
            b_ref[...] = _cumsum_rows(jnp.log(f))
        else:
            q_ref[...] = proj_ref[rs, q_off:q_off + kdim] * (dk ** -0.5)
            k_ref[...] = proj_ref[rs, k_off:k_off + kdim]
            b_ref[...] = _cumsum_rows(gk_ref[rs, :])
        for h in range(heads):
            v = proj_ref[rs, v_off + h * dv:v_off + (h + 1) * dv]
            o, st_new = _head_attention(q_ref, k_ref, b_ref, slice(h * dk, (h + 1) * dk), v,
                                        st_ref[b_idx, h], chunk)
            st_ref[b_idx, h] = st_new
            gate = proj_ref[rs, g_off + h * dv:g_off + (h + 1) * dv]
            o_ref[rs, h * dv:(h + 1) * dv] = _rms_gate(o, ng, gate).astype(BF16)
        return carry

    lax.fori_loop(0, bb * nch, chunk_body, 0)

    y = _dot(o_ref[...], w_out_ref[...])
    out = _layer_norm(ALPHA * x + y, lng_ref[...], lnb_ref[...])
    y_ref[...] = out.reshape(y_ref.shape)

    @pl.when(t == nt - 1)
    def _write_state():
        for b in range(bb):
            for h in range(heads):
                so_ref[b, h] = st_ref[b, h].T


def _const_spec(shape):
    nd = len(shape)
    return pl.BlockSpec(shape, lambda b, t: (0,) * nd, pipeline_mode=pl.Buffered(1))


def _mixer_call(kind, layer, x, s0, weights, *, bb, rows_t, chunk):
    B, T, D = x.shape
    heads, dk, dv = (HGRN_HEADS, HGRN_DK, HGRN_DV) if kind == "hgrn" else (GLA_HEADS, GLA_DK, GLA_DV)
    nch = rows_t // chunk
    rows = bb * rows_t
    grid = (B // bb, T // rows_t)
    has_s0 = s0 is not None

    in_specs = [pl.BlockSpec((bb, rows_t, D), lambda b, t: (b, t, 0))]
    args = [x]
    if has_s0:
        in_specs.append(pl.BlockSpec((bb, heads, dk, dv), lambda b, t: (b, 0, 0, 0)))
        args.append(s0)
    for w in weights:
        in_specs.append(_const_spec(w.shape))
        args.append(w)

    ncols = weights[1].shape[1] if kind == "hgrn" else weights[0].shape[1]
    scratch = [pltpu.VMEM((rows, ncols), F32)]
    if kind == "gla":
        scratch.append(pltpu.VMEM((rows, heads * dk), F32))
    scratch += [
        pltpu.VMEM((chunk, heads * dk), F32),
        pltpu.VMEM((chunk, heads * dk), F32),
        pltpu.VMEM((chunk, heads * dk), F32),
        pltpu.VMEM((rows, heads * dv), BF16),
        pltpu.VMEM((bb, heads, dv, dk), F32),
    ]
    kern = functools.partial(_mixer_kernel, kind=kind, layer=layer, bb=bb, nch=nch, chunk=chunk,
                             has_s0=has_s0)
    return pl.pallas_call(
        kern,
        grid=grid,
        in_specs=in_specs,
        out_specs=[pl.BlockSpec((bb, rows_t, D), lambda b, t: (b, t, 0)),
                   pl.BlockSpec((bb, heads, dk, dv), lambda b, t: (b, 0, 0, 0))],
        out_shape=[jax.ShapeDtypeStruct((B, T, D), F32),
                   jax.ShapeDtypeStruct((B, heads, dk, dv), F32)],
        scratch_shapes=scratch,
        compiler_params=pltpu.CompilerParams(
            dimension_semantics=("arbitrary", "arbitrary"),
            vmem_limit_bytes=VMEM_LIMIT_BYTES),
        name=f"{kind}_mixer_{'s' if has_s0 else 'p'}",
    )(*args)


def _ffn_kernel(*refs, bb, rows_t, has_s0):
    it = iter(refs)
    x_ref = next(it)
    c0_ref = next(it) if has_s0 else None
    w_up_ref = next(it)
    cw_ref = next(it)
    cb_ref = next(it)
    w_dn_ref = next(it)
    lng_ref = next(it)
    lnb_ref = next(it)
    y_ref = next(it)
    co_ref = next(it)
    a_ref = next(it)
    acc_ref = next(it)

    rows = bb * rows_t
    t = pl.program_id(1)
    nt = pl.num_programs(1)
    pad = SUBLANES

    @pl.when(t == 0)
    def _init_carry():
        if has_s0:
            a_ref[:, pad - 2:pad, :] = c0_ref[...]
        else:
            a_ref[:, pad - 2:pad, :] = jnp.zeros((bb, 2, D_FF), F32)

    x = x_ref[...].reshape(rows, D_MODEL)
    xb = x.astype(BF16)

    for n0 in range(0, D_FF, FFN_COLS):
        cs = slice(n0, n0 + FFN_COLS)
        a = _dot(xb, w_up_ref[:, cs])
        a_ref[:, pad:pad + rows_t, cs] = a.reshape(bb, rows_t, FFN_COLS)
        g = _dot(xb, w_up_ref[:, D_FF + n0:D_FF + n0 + FFN_COLS])
        conv = (cb_ref[:, cs]
                + a_ref[:, pad - 2:pad - 2 + rows_t, cs] * cw_ref[0:1, cs]
                + a_ref[:, pad - 1:pad - 1 + rows_t, cs] * cw_ref[1:2, cs]
                + a_ref[:, pad:pad + rows_t, cs] * cw_ref[2:3, cs])
        h = (_gelu_tanh(conv).reshape(rows, FFN_COLS) * g).astype(BF16)
        part = _dot(h, w_dn_ref[cs, :])
        if n0 == 0:
            acc_ref[...] = part
        else:
            acc_ref[...] += part

    out = _layer_norm(ALPHA * x + acc_ref[...], lng_ref[...], lnb_ref[...])
    y_ref[...] = out.reshape(y_ref.shape)

    last2 = a_ref[:, pad + rows_t - 2:pad + rows_t, :]
    a_ref[:, pad - 2:pad, :] = last2

    @pl.when(t == nt - 1)
    def _write_carry():
        co_ref[...] = last2


def _ffn_call(x, c0, weights, *, bb, rows_t):
    B, T, D = x.shape
    grid = (B // bb, T // rows_t)
    has_s0 = c0 is not None
    in_specs = [pl.BlockSpec((bb, rows_t, D), lambda b, t: (b, t, 0))]
    args = [x]
    if has_s0:
        in_specs.append(pl.BlockSpec((bb, CONV_WIDTH - 1, D_FF), lambda b, t: (b, 0, 0)))
        args.append(c0)
    for w in weights:
        in_specs.append(_const_spec(w.shape))
        args.append(w)
    kern = functools.partial(_ffn_kernel, bb=bb, rows_t=rows_t, has_s0=has_s0)
    return pl.pallas_call(
        kern,
        grid=grid,
        in_specs=in_specs,
        out_specs=[pl.BlockSpec((bb, rows_t, D), lambda b, t: (b, t, 0)),
                   pl.BlockSpec((bb, CONV_WIDTH - 1, D_FF), lambda b, t: (b, 0, 0))],
        out_shape=[jax.ShapeDtypeStruct((B, T, D), F32),
                   jax.ShapeDtypeStruct((B, CONV_WIDTH - 1, D_FF), F32)],
        scratch_shapes=[pltpu.VMEM((bb, SUBLANES + rows_t, D_FF), F32),
                        pltpu.VMEM((bb * rows_t, D), F32)],
        compiler_params=pltpu.CompilerParams(
            dimension_semantics=("arbitrary", "arbitrary"),
            vmem_limit_bytes=VMEM_LIMIT_BYTES),
        name=f"conv_ffn_{'s' if has_s0 else 'p'}",
    )(*args)


def _trunk(x, st_hgrn, st_gla, st_conv, layer_weights, *, mixer_bb, ffn_bb, rows_t, chunk):
    new_h, new_g, new_c = [], [], []
    for i in range(DEPTH):
        kind, mix_w, ffn_w = layer_weights[i]
        j = i // 2
        if kind == "hgrn":
            s0 = None if st_hgrn is None else st_hgrn[j]
        else:
            s0 = None if st_gla is None else st_gla[j]
        x, s = _mixer_call(kind, i, x, s0, mix_w, bb=mixer_bb, rows_t=rows_t, chunk=chunk)
        (new_h if kind == "hgrn" else new_g).append(s)
        c0 = None if st_conv is None else st_conv[i]
        x, c = _ffn_call(x, c0, ffn_w, bb=ffn_bb, rows_t=rows_t)
        new_c.append(c)
    return x, jnp.stack(new_h), jnp.stack(new_g), jnp.stack(new_c)


def kernel(x_prompt, x_sample, state_hgrn, state_gla, state_ffn_conv, lb_param, hgrn_w_in, hgrn_w_out, hgrn_norm_g, gla_w_in, gla_w_gk2, gla_b_gk2, gla_w_out, gla_norm_g, ln_mix_g, ln_mix_b, ffn_w_up, ffn_conv_w, ffn_conv_b, ffn_w_down, ln_ffn_g, ln_ffn_b):
    gla_main = 2 * GLA_KEY_DIM + 2 * GLA_VAL_DIM
    layer_weights = []
    for i in range(DEPTH):
        j = i // 2
        if i % 2 == 0:
            mix_w = (lb_param.astype(F32),
                     hgrn_w_in[j].astype(BF16),
                     hgrn_w_out[j].astype(BF16),
                     hgrn_norm_g[j].reshape(1, HGRN_DV),
                     ln_mix_g[i].reshape(1, D_MODEL),
                     ln_mix_b[i].reshape(1, D_MODEL))
            kind = "hgrn"
        else:
            w_r = jnp.pad(gla_w_in[j][:, gla_main:], ((0, 0), (0, LANES - GLA_GATE_RANK)))
            w_gk2 = jnp.pad(gla_w_gk2[j], ((0, LANES - GLA_GATE_RANK), (0, 0)))
            mix_w = (gla_w_in[j][:, :gla_main].astype(BF16),
                     w_r.astype(BF16),
                     w_gk2.astype(BF16),
                     gla_b_gk2[j].reshape(1, GLA_KEY_DIM),
                     gla_w_out[j].astype(BF16),
                     gla_norm_g[j].reshape(1, GLA_DV),
                     ln_mix_g[i].reshape(1, D_MODEL),
                     ln_mix_b[i].reshape(1, D_MODEL))
            kind = "gla"
        ffn_w = (ffn_w_up[i].astype(BF16),
                 ffn_conv_w[i],
                 ffn_conv_b[i].reshape(1, D_FF),
                 ffn_w_down[i].astype(BF16),
                 ln_ffn_g[i].reshape(1, D_MODEL),
                 ln_ffn_b[i].reshape(1, D_MODEL))
        layer_weights.append((kind, mix_w, ffn_w))

    y_p, h_p, g_p, c_p = _trunk(x_prompt, None, None, None, layer_weights,
                                mixer_bb=1, ffn_bb=1, rows_t=PROMPT_ROWS, chunk=CHUNK)
    dec_b, dec_t = x_sample.shape[0], x_sample.shape[1]
    y_s, h_s, g_s, c_s = _trunk(x_sample, state_hgrn, state_gla, state_ffn_conv, layer_weights,
                                mixer_bb=8, ffn_bb=dec_b, rows_t=dec_t, chunk=dec_t)
    return (y_p, y_s, h_p, h_s, g_p, g_s, c_p, c_s)
```

```python
import functools
import math

import jax
import jax.numpy as jnp
from jax import lax
from jax.experimental import pallas as pl
from jax.experimental.pallas import tpu as pltpu

F32 = jnp.float32
BF16 = jnp.bfloat16

D_MODEL = 1024
DEPTH = 2
CHUNK = 64
HGRN_HEADS = 8
HGRN_DK = 128
HGRN_DV = 128
HGRN_FDIM = HGRN_HEADS * HGRN_DK
GLA_HEADS = 4
GLA_KEY_DIM = 512
GLA_VAL_DIM = 1024
GLA_DK = 128
GLA_DV = 256
GLA_GATE_RANK = 16
GLA_GATE_NORMALIZER = 16.0
D_FF = 2816
CONV_WIDTH = 3
ALPHA = (2.0 * DEPTH) ** 0.25
LN_EPS = 1e-5
RMS_EPS = 1e-6
LOG2E = 1.4426950408889634

SUBLANES = 8
LANES = 128
VMEM_LIMIT_BYTES = 56 * 1024 * 1024
PROMPT_ROWS = 512
FFN_COLS = 256
MAX_BLOCK_DECAY_LOG2 = 64.0
CHUNKS_PER_TRIP = 2


def _dot(a, b):
    return jnp.dot(a, b, preferred_element_type=F32)


def _dot_nt(a, b):
    return lax.dot_general(a, b, (((1,), (1,)), ((), ())), preferred_element_type=F32)


def _dot_tn(a, b):
    return lax.dot_general(a, b, (((0,), (0,)), ((), ())), preferred_element_type=F32)


def _sigmoid(x):
    return 1.0 / (1.0 + jnp.exp(-x))


def _silu(x):
    return x * _sigmoid(x)


def _gelu_tanh(x):
    c = math.sqrt(2.0 / math.pi)
    k = -2.0 * c * LOG2E
    z = x * ((k * 0.044715) * (x * x) + k)
    return x / (1.0 + jnp.exp2(z))


def _log2_sigmoid(x):
    return jnp.minimum(x, 0.0) * LOG2E - jnp.log2(1.0 + jnp.exp2(jnp.abs(x) * (-LOG2E)))


def _layer_norm(x, g, b):
    mu = jnp.mean(x, axis=-1, keepdims=True)
    xc = x - mu
    var = jnp.mean(xc * xc, axis=-1, keepdims=True)
    return xc * lax.rsqrt(var + LN_EPS) * g + b


def _cumsum_rows(g):
    c = g.shape[0]
    r = lax.broadcasted_iota(jnp.int32, (c, c), 0)
    s = lax.broadcasted_iota(jnp.int32, (c, c), 1)
    tri = (s <= r).astype(BF16)
    hi = g.astype(BF16)
    r1 = g - hi.astype(F32)
    mid = r1.astype(BF16)
    lo = (r1 - mid.astype(F32)).astype(BF16)
    return _dot(tri, hi) + _dot(tri, mid) + _dot(tri, lo)


def _head_attention(q_view, k_view, b_view, bc_ref, kc_ref, r0, h, dk, v, chunk, exact_diag):
    nb = chunk // SUBLANES

    def cols(view):
        return slice(view[1] + h * dk, view[1] + (h + 1) * dk)

    def blk(view, i):
        return view[0][pl.ds(pl.multiple_of(r0 + SUBLANES * i, SUBLANES), SUBLANES), cols(view)]

    def row(ref, t):
        return jnp.broadcast_to(ref[t:t + 1, h * dk:(h + 1) * dk], (SUBLANES, dk))

    def full(view):
        return view[0][pl.ds(r0, chunk), cols(view)]

    def b_end(j):
        return row(bc_ref, SUBLANES * j + SUBLANES - 1)

    kpp = jnp.concatenate(
        [blk(k_view, j) * jnp.exp2(b_end(j) - blk(b_view, j)) for j in range(nb)],
        axis=0).astype(BF16)

    col = lax.broadcasted_iota(jnp.int32, (SUBLANES, chunk), 1)
    rw = lax.broadcasted_iota(jnp.int32, (SUBLANES, chunk), 0)
    colblk = col // SUBLANES

    lq, pair = [], {}
    for i in range(nb):
        qi, bi = blk(q_view, i), blk(b_view, i)
        for j in range(i if exact_diag else i + 1):
            pair[(i, j)] = len(lq)
            lq.append(qi * jnp.exp2(bi - b_end(j)))
    if lq:
        cross = _dot_nt(jnp.concatenate(lq, axis=0).astype(BF16), kpp)

    def cross_blk(i, j):
        p = pair[(i, j)]
        return cross[SUBLANES * p:SUBLANES * (p + 1), :]

    b_all = full(b_view)
    b_last = bc_ref[chunk - 1:chunk, h * dk:(h + 1) * dk]
    vb = v.astype(BF16)
    qe = (full(q_view) * jnp.exp2(b_all)).astype(BF16)
    kd = (full(k_view) * jnp.exp2(b_last - b_all)).astype(BF16)
    kv = _dot_tn(vb, kd)

    def finish(st):
        a_rows = []
        for i in range(nb):
            if exact_diag:
                acc = jnp.zeros((SUBLANES, chunk), F32)
                for j in range(i):
                    acc = jnp.where(colblk == j, cross_blk(i, j), acc)
                qi, bi = blk(q_view, i), blk(b_view, i)
                for s in range(SUBLANES):
                    t = SUBLANES * i + s
                    e = jnp.exp2(jnp.minimum(bi - row(bc_ref, t), 0.0))
                    cs = jnp.sum(qi * e * row(kc_ref, t), axis=-1, keepdims=True)
                    acc = jnp.where(col == t, cs, acc)
            else:
                acc = cross_blk(i, 0)
                for j in range(1, i + 1):
                    acc = jnp.where(colblk == j, cross_blk(i, j), acc)
            a_rows.append(jnp.where(col <= SUBLANES * i + rw, acc, 0.0))
        a = jnp.concatenate(a_rows, axis=0).astype(BF16)
        o = _dot(a, vb) + _dot_nt(qe, st.astype(BF16))
        st_new = jnp.exp2(b_last) * st + kv
        return o, st_new

    return finish


def _rms_gate(o, norm_g, gate):
    ms = jnp.mean(o * o, axis=-1, keepdims=True)
    return o * lax.rsqrt(ms + RMS_EPS) * norm_g * _silu(gate)


def _mixer_kernel(*refs, kind, layer, bb, nch, chunk, has_s0):
    it = iter(refs)
    x_ref = next(it)
    s0_ref = next(it) if has_s0 else None
    if kind == "hgrn":
        lb_ref = next(it)
        w_in_ref = next(it)
    else:
        w_in_ref = next(it)
        w_r_ref = next(it)
        w_gk2_ref = next(it)
        b_gk2_ref = next(it)
    w_out_ref = next(it)
    ng_ref = next(it)
    lng_ref = next(it)
    lnb_ref = next(it)
    y_ref = next(it)
    so_ref = next(it)
    proj_ref = next(it)
    aux_ref = next(it)
    bc_ref = next(it)
    kc_ref = next(it)
    o_ref = next(it)
    st_ref = next(it)

    heads, dk, dv = (HGRN_HEADS, HGRN_DK, HGRN_DV) if kind == "hgrn" else (GLA_HEADS, GLA_DK, GLA_DV)
    kdim = heads * dk
    rows = bb * nch * chunk
    nb = chunk // SUBLANES
    t = pl.program_id(1)
    nt = pl.num_programs(1)

    @pl.when(t == 0)
    def _init_state():
        for b in range(bb):
            for h in range(heads):
                if has_s0:
                    st_ref[b, h] = s0_ref[b, h].T
                else:
                    st_ref[b, h] = jnp.zeros((dv, dk), F32)

    x = x_ref[...].reshape(rows, D_MODEL)
    xb = x.astype(BF16)
    ncols = proj_ref.shape[1]
    step = 512
    for n0 in range(0, ncols, step):
        proj_ref[:, n0:n0 + step] = _dot(xb, w_in_ref[:, n0:n0 + step])

    if kind == "hgrn":
        p = lb_ref[...]
        pe = jnp.exp(p - jnp.max(p, axis=0, keepdims=True))
        lb = jnp.sum(pe[:layer + 1], axis=0, keepdims=True) / jnp.sum(pe, axis=0, keepdims=True)
        q_off, f_off, v_off, g_off = 0, kdim, 2 * kdim, 3 * kdim
        q_view, k_view, b_view = (proj_ref, q_off), (aux_ref, 0), (proj_ref, f_off)
    else:
        r = _dot(xb, w_r_ref[...]).astype(BF16)
        z = _dot(r, w_gk2_ref[...]) + b_gk2_ref[...]
        aux_ref[...] = _log2_sigmoid(z) * (1.0 / GLA_GATE_NORMALIZER)
        q_off, k_off, v_off, g_off = 0, kdim, 2 * kdim, 2 * kdim + heads * dv
        q_view, k_view, b_view = (proj_ref, q_off), (proj_ref, k_off), (aux_ref, 0)

    def gate_body(ci, m):
        rs = pl.ds(pl.multiple_of(ci * chunk, chunk), chunk)
        qc = slice(q_off, q_off + kdim)
        if kind == "hgrn":
            fc = slice(f_off, f_off + kdim)
            proj_ref[rs, qc] = _silu(proj_ref[rs, qc])
            f = lb + (1.0 - lb) * _sigmoid(proj_ref[rs, fc])
            aux_ref[rs, :] = 1.0 - f
            b = _cumsum_rows(jnp.log2(f))
            proj_ref[rs, fc] = b
        else:
            proj_ref[rs, qc] = proj_ref[rs, qc] * (dk ** -0.5)
            b = _cumsum_rows(aux_ref[rs, :])
            aux_ref[rs, :] = b
        m = jnp.maximum(m, -b[0:SUBLANES])
        for j in range(1, nb):
            m = jnp.maximum(m, b[SUBLANES * (j - 1):SUBLANES * j] - b[SUBLANES * j:SUBLANES * (j + 1)])
        return m

    m = lax.fori_loop(0, bb * nch, gate_body, jnp.zeros((SUBLANES, kdim), F32), unroll=2)
    bounded = jnp.max(m) <= MAX_BLOCK_DECAY_LOG2

    ng = ng_ref[...]

    def chunk_loop(exact_diag):
        def first_stage(ci, slot):
            r0 = pl.multiple_of(ci * chunk, chunk)
            rs = pl.ds(r0, chunk)
            bc, kc = bc_ref.at[slot], kc_ref.at[slot]
            bc[...] = b_view[0][rs, b_view[1]:b_view[1] + kdim]
            if exact_diag:
                kc[...] = k_view[0][rs, k_view[1]:k_view[1] + kdim]
            return [_head_attention(q_view, k_view, b_view, bc, kc, r0, h, dk,
                                    proj_ref[rs, v_off + h * dv:v_off + (h + 1) * dv],
                                    chunk, exact_diag) for h in range(heads)]

        def second_stage(ci, finishers):
            rs = pl.ds(pl.multiple_of(ci * chunk, chunk), chunk)
            b_idx = ci // nch
            for h, finish in enumerate(finishers):
                o, st_new = finish(st_ref[b_idx, h])
                st_ref[b_idx, h] = st_new
                gate = proj_ref[rs, g_off + h * dv:g_off + (h + 1) * dv]
                o_ref[rs, h * dv:(h + 1) * dv] = _rms_gate(o, ng, gate).astype(BF16)

        per_trip = 1 if exact_diag else CHUNKS_PER_TRIP

        def trip(ti, carry):
            staged = [first_stage(ti * per_trip + slot, slot) for slot in range(per_trip)]
            for slot in range(per_trip):
                second_stage(ti * per_trip + slot, staged[slot])
            return carry
        lax.fori_loop(0, (bb * nch) // per_trip, trip, 0)

    @pl.when(bounded)
    def _matmul_diag():
        chunk_loop(False)

    @pl.when(jnp.logical_not(bounded))
    def _pairwise_diag():
        chunk_loop(True)

    y = _dot(o_ref[...], w_out_ref[...])
    out = _layer_norm(ALPHA * x + y, lng_ref[...], lnb_ref[...])
    y_ref[...] = out.reshape(y_ref.shape)

    @pl.when(t == nt - 1)
    def _write_state():
        for b in range(bb):
            for h in range(heads):
                so_ref[b, h] = st_ref[b, h].T


def _const_spec(shape):
    nd = len(shape)
    return pl.BlockSpec(shape, lambda b, t: (0,) * nd, pipeline_mode=pl.Buffered(1))


def _mixer_call(kind, layer, x, s0, weights, *, bb, rows_t, chunk):
    B, T, D = x.shape
    heads, dk, dv = (HGRN_HEADS, HGRN_DK, HGRN_DV) if kind == "hgrn" else (GLA_HEADS, GLA_DK, GLA_DV)
    nch = rows_t // chunk
    rows = bb * rows_t
    grid = (B // bb, T // rows_t)
    has_s0 = s0 is not None

    in_specs = [pl.BlockSpec((bb, rows_t, D), lambda b, t: (b, t, 0))]
    args = [x]
    if has_s0:
        in_specs.append(pl.BlockSpec((bb, heads, dk, dv), lambda b, t: (b, 0, 0, 0)))
        args.append(s0)
    for w in weights:
        in_specs.append(_const_spec(w.shape))
        args.append(w)

    ncols = weights[1].shape[1] if kind == "hgrn" else weights[0].shape[1]
    scratch = [
        pltpu.VMEM((rows, ncols), F32),
        pltpu.VMEM((rows, heads * dk), F32),
        pltpu.VMEM((CHUNKS_PER_TRIP, chunk, heads * dk), F32),
        pltpu.VMEM((CHUNKS_PER_TRIP, chunk, heads * dk), F32),
        pltpu.VMEM((rows, heads * dv), BF16),
        pltpu.VMEM((bb, heads, dv, dk), F32),
    ]
    kern = functools.partial(_mixer_kernel, kind=kind, layer=layer, bb=bb, nch=nch, chunk=chunk,
                             has_s0=has_s0)
    return pl.pallas_call(
        kern,
        grid=grid,
        in_specs=in_specs,
        out_specs=[pl.BlockSpec((bb, rows_t, D), lambda b, t: (b, t, 0)),
                   pl.BlockSpec((bb, heads, dk, dv), lambda b, t: (b, 0, 0, 0))],
        out_shape=[jax.ShapeDtypeStruct((B, T, D), F32),
                   jax.ShapeDtypeStruct((B, heads, dk, dv), F32)],
        scratch_shapes=scratch,
        compiler_params=pltpu.CompilerParams(
            dimension_semantics=("arbitrary", "arbitrary"),
            vmem_limit_bytes=VMEM_LIMIT_BYTES),
        name=f"{kind}_mixer_{'s' if has_s0 else 'p'}",
    )(*args)


def _ffn_kernel(*refs, bb, rows_t, has_s0):
    it = iter(refs)
    x_ref = next(it)
    c0_ref = next(it) if has_s0 else None
    w_up_ref = next(it)
    cw_ref = next(it)
    cb_ref = next(it)
    w_dn_ref = next(it)
    lng_ref = next(it)
    lnb_ref = next(it)
    y_ref = next(it)
    co_ref = next(it)
    a_ref = next(it)
    acc_ref = next(it)

    rows = bb * rows_t
    t = pl.program_id(1)
    nt = pl.num_programs(1)
    pad = SUBLANES

    @pl.when(t == 0)
    def _init_carry():
        if has_s0:
            a_ref[:, pad - 2:pad, :] = c0_ref[...]
        else:
            a_ref[:, pad - 2:pad, :] = jnp.zeros((bb, 2, D_FF), F32)

    x = x_ref[...].reshape(rows, D_MODEL)
    xb = x.astype(BF16)

    def up_proj(n0):
        cs = slice(n0, n0 + FFN_COLS)
        a = _dot(xb, w_up_ref[:, cs])
        a_ref[:, pad:pad + rows_t, cs] = a.reshape(bb, rows_t, FFN_COLS)
        return _dot(xb, w_up_ref[:, D_FF + n0:D_FF + n0 + FFN_COLS])

    def act_down(n0, g):
        cs = slice(n0, n0 + FFN_COLS)
        conv = (cb_ref[:, cs]
                + a_ref[:, pad - 2:pad - 2 + rows_t, cs] * cw_ref[0:1, cs]
                + a_ref[:, pad - 1:pad - 1 + rows_t, cs] * cw_ref[1:2, cs]
                + a_ref[:, pad:pad + rows_t, cs] * cw_ref[2:3, cs])
        h = (_gelu_tanh(conv).reshape(rows, FFN_COLS) * g).astype(BF16)
        part = _dot(h, w_dn_ref[cs, :])
        if n0 == 0:
            acc_ref[...] = part
        else:
            acc_ref[...] += part

    starts = list(range(0, D_FF, FFN_COLS))
    g_next = up_proj(starts[0])
    for idx, n0 in enumerate(starts):
        g = g_next
        if idx + 1 < len(starts):
            g_next = up_proj(starts[idx + 1])
        act_down(n0, g)

    out = _layer_norm(ALPHA * x + acc_ref[...], lng_ref[...], lnb_ref[...])
    y_ref[...] = out.reshape(y_ref.shape)

    last2 = a_ref[:, pad + rows_t - 2:pad + rows_t, :]
    a_ref[:, pad - 2:pad, :] = last2

    @pl.when(t == nt - 1)
    def _write_carry():
        co_ref[...] = last2


def _ffn_call(x, c0, weights, *, bb, rows_t):
    B, T, D = x.shape
    grid = (B // bb, T // rows_t)
    has_s0 = c0 is not None
    in_specs = [pl.BlockSpec((bb, rows_t, D), lambda b, t: (b, t, 0))]
    args = [x]
    if has_s0:
        in_specs.append(pl.BlockSpec((bb, CONV_WIDTH - 1, D_FF), lambda b, t: (b, 0, 0)))
        args.append(c0)
    for w in weights:
        in_specs.append(_const_spec(w.shape))
        args.append(w)
    kern = functools.partial(_ffn_kernel, bb=bb, rows_t=rows_t, has_s0=has_s0)
    return pl.pallas_call(
        kern,
        grid=grid,
        in_specs=in_specs,
        out_specs=[pl.BlockSpec((bb, rows_t, D), lambda b, t: (b, t, 0)),
                   pl.BlockSpec((bb, CONV_WIDTH - 1, D_FF), lambda b, t: (b, 0, 0))],
        out_shape=[jax.ShapeDtypeStruct((B, T, D), F32),
                   jax.ShapeDtypeStruct((B, CONV_WIDTH - 1, D_FF), F32)],
        scratch_shapes=[pltpu.VMEM((bb, SUBLANES + rows_t, D_FF), F32),
                        pltpu.VMEM((bb * rows_t, D), F32)],
        compiler_params=pltpu.CompilerParams(
            dimension_semantics=("arbitrary", "arbitrary"),
            vmem_limit_bytes=VMEM_LIMIT_BYTES),
        name=f"conv_ffn_{'s' if has_s0 else 'p'}",
    )(*args)


def _trunk(x, st_hgrn, st_gla, st_conv, layer_weights, *, mixer_bb, ffn_bb, rows_t, chunk):
    new_h, new_g, new_c = [], [], []
    for i in range(DEPTH):
        kind, mix_w, ffn_w = layer_weights[i]
        j = i // 2
        if kind == "hgrn":
            s0 = None if st_hgrn is None else st_hgrn[j]
        else:
            s0 = None if st_gla is None else st_gla[j]
        x, s = _mixer_call(kind, i, x, s0, mix_w, bb=mixer_bb, rows_t=rows_t, chunk=chunk)
        (new_h if kind == "hgrn" else new_g).append(s)
        c0 = None if st_conv is None else st_conv[i]
        x, c = _ffn_call(x, c0, ffn_w, bb=ffn_bb, rows_t=rows_t)
        new_c.append(c)
    return x, jnp.stack(new_h), jnp.stack(new_g), jnp.stack(new_c)


def kernel(x_prompt, x_sample, state_hgrn, state_gla, state_ffn_conv, lb_param, hgrn_w_in, hgrn_w_out, hgrn_norm_g, gla_w_in, gla_w_gk2, gla_b_gk2, gla_w_out, gla_norm_g, ln_mix_g, ln_mix_b, ffn_w_up, ffn_conv_w, ffn_conv_b, ffn_w_down, ln_ffn_g, ln_ffn_b):
    gla_main = 2 * GLA_KEY_DIM + 2 * GLA_VAL_DIM
    layer_weights = []
    for i in range(DEPTH):
        j = i // 2
        if i % 2 == 0:
            mix_w = (lb_param.astype(F32),
                     hgrn_w_in[j].astype(BF16),
                     hgrn_w_out[j].astype(BF16),
                     hgrn_norm_g[j].reshape(1, HGRN_DV),
                     ln_mix_g[i].reshape(1, D_MODEL),
                     ln_mix_b[i].reshape(1, D_MODEL))
            kind = "hgrn"
        else:
            w_r = jnp.pad(gla_w_in[j][:, gla_main:], ((0, 0), (0, LANES - GLA_GATE_RANK)))
            w_gk2 = jnp.pad(gla_w_gk2[j], ((0, LANES - GLA_GATE_RANK), (0, 0)))
            mix_w = (gla_w_in[j][:, :gla_main].astype(BF16),
                     w_r.astype(BF16),
                     w_gk2.astype(BF16),
                     gla_b_gk2[j].reshape(1, GLA_KEY_DIM),
                     gla_w_out[j].astype(BF16),
                     gla_norm_g[j].reshape(1, GLA_DV),
                     ln_mix_g[i].reshape(1, D_MODEL),
                     ln_mix_b[i].reshape(1, D_MODEL))
            kind = "gla"
        ffn_w = (ffn_w_up[i].astype(BF16),
                 ffn_conv_w[i],
                 ffn_conv_b[i].reshape(1, D_FF),
                 ffn_w_down[i].astype(BF16),
                 ln_ffn_g[i].reshape(1, D_MODEL),
                 ln_ffn_b[i].reshape(1, D_MODEL))
        layer_weights.append((kind, mix_w, ffn_w))

    y_p, h_p, g_p, c_p = _trunk(x_prompt, None, None, None, layer_weights,
                                mixer_bb=1, ffn_bb=1, rows_t=PROMPT_ROWS, chunk=CHUNK)
    dec_b, dec_t = x_sample.shape[0], x_sample.shape[1]
    y_s, h_s, g_s, c_s = _trunk(x_sample, state_hgrn, state_gla, state_ffn_conv, layer_weights,
                                mixer_bb=8, ffn_bb=dec_b, rows_t=dec_t, chunk=dec_t)
    return (y_p, y_s, h_p, h_s, g_p, g_s, c_p, c_s)
```

```python
import functools
import math

import jax
import jax.numpy as jnp
from jax import lax
from jax.experimental import pallas as pl
from jax.experimental.pallas import tpu as pltpu

F32 = jnp.float32
BF16 = jnp.bfloat16

D_MODEL = 1024
DEPTH = 2
CHUNK = 64
HGRN_HEADS = 8
HGRN_DK = 128
HGRN_DV = 128
HGRN_FDIM = HGRN_HEADS * HGRN_DK
GLA_HEADS = 4
GLA_KEY_DIM = 512
GLA_VAL_DIM = 1024
GLA_DK = 128
GLA_DV = 256
GLA_GATE_RANK = 16
GLA_GATE_NORMALIZER = 16.0
D_FF = 2816
CONV_WIDTH = 3
ALPHA = (2.0 * DEPTH) ** 0.25
LN_EPS = 1e-5
RMS_EPS = 1e-6
LOG2E = 1.4426950408889634

SUBLANES = 8
LANES = 128
VMEM_LIMIT_BYTES = 56 * 1024 * 1024
PROMPT_ROWS = 512
FFN_COLS = 256
DOWN_COLS = 512
PROJ_COLS = 512
MAX_WINDOW_DECAY_LOG2 = 50.0
FAST_BLOCK_ROWS = 2 * SUBLANES
CHUNKS_PER_TRIP = 2


def _dot(a, b):
    return jnp.dot(a, b, preferred_element_type=F32)


def _dot_nt(a, b):
    return lax.dot_general(a, b, (((1,), (1,)), ((), ())), preferred_element_type=F32)


def _dot_tn(a, b):
    return lax.dot_general(a, b, (((0,), (0,)), ((), ())), preferred_element_type=F32)


def _sigmoid(x):
    return 1.0 / (1.0 + jnp.exp(-x))


def _silu(x):
    return x * _sigmoid(x)


def _gelu_tanh(x):
    c = math.sqrt(2.0 / math.pi)
    k = -2.0 * c * LOG2E
    z = x * ((k * 0.044715) * (x * x) + k)
    return x / (1.0 + jnp.exp2(z))


def _log2_sigmoid(x):
    return jnp.minimum(x, 0.0) * LOG2E - jnp.log2(1.0 + jnp.exp2(jnp.abs(x) * (-LOG2E)))


def _layer_norm(x, g, b):
    mu = jnp.mean(x, axis=-1, keepdims=True)
    xc = x - mu
    var = jnp.mean(xc * xc, axis=-1, keepdims=True)
    return xc * lax.rsqrt(var + LN_EPS) * g + b


def _cumsum_rows(g):
    c = g.shape[0]
    r = lax.broadcasted_iota(jnp.int32, (c, c), 0)
    s = lax.broadcasted_iota(jnp.int32, (c, c), 1)
    tri = (s <= r).astype(BF16)
    hi = g.astype(BF16)
    r1 = g - hi.astype(F32)
    mid = r1.astype(BF16)
    lo = (r1 - mid.astype(F32)).astype(BF16)
    return _dot(tri, hi) + _dot(tri, mid) + _dot(tri, lo)


def _head_attention(q_view, k_view, b_view, bc_ref, kc_ref, r0, h, dk, v, chunk, exact_diag):
    R = SUBLANES if exact_diag else min(FAST_BLOCK_ROWS, chunk)
    nb = chunk // R

    def cols(view):
        return slice(view[1] + h * dk, view[1] + (h + 1) * dk)

    def blk(view, i):
        return view[0][pl.ds(pl.multiple_of(r0 + R * i, R), R), cols(view)]

    def row(ref, t):
        return jnp.broadcast_to(ref[t:t + 1, h * dk:(h + 1) * dk], (R, dk))

    def full(view):
        return view[0][pl.ds(r0, chunk), cols(view)]

    def b_end(j):
        return row(bc_ref, R * j + R - 1)

    kpp = jnp.concatenate(
        [blk(k_view, j) * jnp.exp2(b_end(j) - blk(b_view, j)) for j in range(nb)],
        axis=0).astype(BF16)

    col = lax.broadcasted_iota(jnp.int32, (R, chunk), 1)
    rw = lax.broadcasted_iota(jnp.int32, (R, chunk), 0)
    colblk = col // R

    lq, pair = [], {}
    for i in range(nb):
        qi, bi = blk(q_view, i), blk(b_view, i)
        for j in range(i if exact_diag else i + 1):
            pair[(i, j)] = len(lq)
            lq.append(qi * jnp.exp2(bi - b_end(j)))
    if lq:
        cross = _dot_nt(jnp.concatenate(lq, axis=0).astype(BF16), kpp)

    def cross_blk(i, j):
        p = pair[(i, j)]
        return cross[R * p:R * (p + 1), :]

    b_all = full(b_view)
    b_last = bc_ref[chunk - 1:chunk, h * dk:(h + 1) * dk]
    vb = v.astype(BF16)
    qe = (full(q_view) * jnp.exp2(b_all)).astype(BF16)
    kd = (full(k_view) * jnp.exp2(b_last - b_all)).astype(BF16)
    kv = _dot_tn(vb, kd)

    def finish(st):
        a_rows = []
        for i in range(nb):
            if exact_diag:
                acc = jnp.zeros((R, chunk), F32)
                for j in range(i):
                    acc = jnp.where(colblk == j, cross_blk(i, j), acc)
                qi, bi = blk(q_view, i), blk(b_view, i)
                for s in range(R):
                    t = R * i + s
                    e = jnp.exp2(jnp.minimum(bi - row(bc_ref, t), 0.0))
                    cs = jnp.sum(qi * e * row(kc_ref, t), axis=-1, keepdims=True)
                    acc = jnp.where(col == t, cs, acc)
            else:
                acc = cross_blk(i, 0)
                for j in range(1, i + 1):
                    acc = jnp.where(colblk == j, cross_blk(i, j), acc)
            a_rows.append(jnp.where(col <= R * i + rw, acc, 0.0))
        a = jnp.concatenate(a_rows, axis=0).astype(BF16)
        o = _dot(a, vb) + _dot_nt(qe, st.astype(BF16))
        st_new = jnp.exp2(b_last) * st + kv
        return o, st_new

    return finish


def _rms_gate(o, norm_g, gate_act):
    ms = jnp.mean(o * o, axis=-1, keepdims=True)
    return o * lax.rsqrt(ms + RMS_EPS) * norm_g * gate_act


def _mixer_kernel(*refs, kind, layer, bb, nch, chunk, has_s0):
    it = iter(refs)
    x_ref = next(it)
    s0_ref = next(it) if has_s0 else None
    if kind == "hgrn":
        lb_ref = next(it)
        w_in_ref = next(it)
    else:
        w_in_ref = next(it)
        w_r_ref = next(it)
        w_gk2_ref = next(it)
        b_gk2_ref = next(it)
    w_out_ref = next(it)
    ng_ref = next(it)
    lng_ref = next(it)
    lnb_ref = next(it)
    y_ref = next(it)
    so_ref = next(it)
    proj_ref = next(it)
    aux_ref = next(it)
    bc_ref = next(it)
    kc_ref = next(it)
    o_ref = next(it)
    st_ref = next(it)

    heads, dk, dv = (HGRN_HEADS, HGRN_DK, HGRN_DV) if kind == "hgrn" else (GLA_HEADS, GLA_DK, GLA_DV)
    kdim = heads * dk
    rows = bb * nch * chunk
    nb = chunk // SUBLANES
    t = pl.program_id(1)
    nt = pl.num_programs(1)

    @pl.when(t == 0)
    def _init_state():
        for b in range(bb):
            for h in range(heads):
                if has_s0:
                    st_ref[b, h] = s0_ref[b, h].T
                else:
                    st_ref[b, h] = jnp.zeros((dv, dk), F32)

    x = x_ref[...].reshape(rows, D_MODEL)
    xb = x.astype(BF16)
    ncols = proj_ref.shape[1]
    step = PROJ_COLS
    decay_max = []

    def cumsum_in_place(ref, cs):
        m = jnp.zeros((SUBLANES, cs.stop - cs.start), F32)
        for c in range(bb * nch):
            rs = slice(c * chunk, (c + 1) * chunk)
            b = _cumsum_rows(ref[rs, cs])
            ref[rs, cs] = b
            m = jnp.maximum(m, -b[0:SUBLANES])
            for j in range(1, nb):
                m = jnp.maximum(m, b[SUBLANES * (j - 1):SUBLANES * j] - b[SUBLANES * j:SUBLANES * (j + 1)])
        decay_max.append(jnp.max(m))

    if kind == "hgrn":
        p = lb_ref[...]
        pe = jnp.exp(p - jnp.max(p, axis=0, keepdims=True))
        lb = jnp.sum(pe[:layer + 1], axis=0, keepdims=True) / jnp.sum(pe, axis=0, keepdims=True)
        q_off, f_off, v_off, g_off = 0, kdim, 2 * kdim, 3 * kdim
        q_view, k_view, b_view = (proj_ref, q_off), (aux_ref, 0), (proj_ref, f_off)
    else:
        q_off, k_off, v_off, g_off = 0, kdim, 2 * kdim, 2 * kdim + heads * dv
        q_view, k_view, b_view = (proj_ref, q_off), (proj_ref, k_off), (aux_ref, 0)

    for n0 in range(0, ncols, step):
        cs = slice(n0, n0 + step)
        blk = _dot(xb, w_in_ref[:, cs])
        if n0 >= g_off:
            proj_ref[:, cs] = _silu(blk)
        elif n0 >= v_off:
            proj_ref[:, cs] = blk
        elif kind == "hgrn" and n0 >= f_off:
            fs = slice(n0 - f_off, n0 - f_off + step)
            f = lb[:, fs] + (1.0 - lb[:, fs]) * _sigmoid(blk)
            aux_ref[:, fs] = 1.0 - f
            proj_ref[:, cs] = jnp.log2(f)
            cumsum_in_place(proj_ref, cs)
        elif kind == "hgrn":
            proj_ref[:, cs] = _silu(blk)
        elif n0 >= k_off:
            proj_ref[:, cs] = blk
        else:
            proj_ref[:, cs] = blk * (dk ** -0.5)

    if kind == "gla":
        r = _dot(xb, w_r_ref[...]).astype(BF16)
        z = _dot(r, w_gk2_ref[...]) + b_gk2_ref[...]
        aux_ref[...] = _log2_sigmoid(z) * (1.0 / GLA_GATE_NORMALIZER)
        cumsum_in_place(aux_ref, slice(0, kdim))

    bounded = functools.reduce(jnp.maximum, decay_max) <= MAX_WINDOW_DECAY_LOG2

    ng = ng_ref[...]

    def chunk_loop(exact_diag):
        def first_stage(ci, slot):
            r0 = pl.multiple_of(ci * chunk, chunk)
            rs = pl.ds(r0, chunk)
            bc, kc = bc_ref.at[slot], kc_ref.at[slot]
            bc[...] = b_view[0][rs, b_view[1]:b_view[1] + kdim]
            if exact_diag:
                kc[...] = k_view[0][rs, k_view[1]:k_view[1] + kdim]
            return [_head_attention(q_view, k_view, b_view, bc, kc, r0, h, dk,
                                    proj_ref[rs, v_off + h * dv:v_off + (h + 1) * dv],
                                    chunk, exact_diag) for h in range(heads)]

        def second_stage(ci, finishers):
            rs = pl.ds(pl.multiple_of(ci * chunk, chunk), chunk)
            b_idx = ci // nch
            for h, finish in enumerate(finishers):
                o, st_new = finish(st_ref[b_idx, h])
                st_ref[b_idx, h] = st_new
                gate = proj_ref[rs, g_off + h * dv:g_off + (h + 1) * dv]
                o_ref[rs, h * dv:(h + 1) * dv] = _rms_gate(o, ng, gate).astype(BF16)

        per_trip = 1 if exact_diag else CHUNKS_PER_TRIP

        def trip(ti, carry):
            staged = [first_stage(ti * per_trip + slot, slot) for slot in range(per_trip)]
            for slot in range(per_trip):
                second_stage(ti * per_trip + slot, staged[slot])
            return carry
        lax.fori_loop(0, (bb * nch) // per_trip, trip, 0)

    @pl.when(bounded)
    def _matmul_diag():
        chunk_loop(False)

    @pl.when(jnp.logical_not(bounded))
    def _pairwise_diag():
        chunk_loop(True)

    y = _dot(o_ref[...], w_out_ref[...])
    out = _layer_norm(ALPHA * x + y, lng_ref[...], lnb_ref[...])
    y_ref[...] = out.reshape(y_ref.shape)

    @pl.when(t == nt - 1)
    def _write_state():
        for b in range(bb):
            for h in range(heads):
                so_ref[b, h] = st_ref[b, h].T


def _const_spec(shape):
    nd = len(shape)
    return pl.BlockSpec(shape, lambda b, t: (0,) * nd, pipeline_mode=pl.Buffered(1))


def _mixer_call(kind, layer, x, s0, weights, *, bb, rows_t, chunk):
    B, T, D = x.shape
    heads, dk, dv = (HGRN_HEADS, HGRN_DK, HGRN_DV) if kind == "hgrn" else (GLA_HEADS, GLA_DK, GLA_DV)
    nch = rows_t // chunk
    rows = bb * rows_t
    grid = (B // bb, T // rows_t)
    has_s0 = s0 is not None

    in_specs = [pl.BlockSpec((bb, rows_t, D), lambda b, t: (b, t, 0))]
    args = [x]
    if has_s0:
        in_specs.append(pl.BlockSpec((bb, heads, dk, dv), lambda b, t: (b, 0, 0, 0)))
        args.append(s0)
    for w in weights:
        in_specs.append(_const_spec(w.shape))
        args.append(w)

    ncols = weights[1].shape[1] if kind == "hgrn" else weights[0].shape[1]
    scratch = [
        pltpu.VMEM((rows, ncols), F32),
        pltpu.VMEM((rows, heads * dk), F32),
        pltpu.VMEM((CHUNKS_PER_TRIP, chunk, heads * dk), F32),
        pltpu.VMEM((CHUNKS_PER_TRIP, chunk, heads * dk), F32),
        pltpu.VMEM((rows, heads * dv), BF16),
        pltpu.VMEM((bb, heads, dv, dk), F32),
    ]
    kern = functools.partial(_mixer_kernel, kind=kind, layer=layer, bb=bb, nch=nch, chunk=chunk,
                             has_s0=has_s0)
    return pl.pallas_call(
        kern,
        grid=grid,
        in_specs=in_specs,
        out_specs=[pl.BlockSpec((bb, rows_t, D), lambda b, t: (b, t, 0)),
                   pl.BlockSpec((bb, heads, dk, dv), lambda b, t: (b, 0, 0, 0))],
        out_shape=[jax.ShapeDtypeStruct((B, T, D), F32),
                   jax.ShapeDtypeStruct((B, heads, dk, dv), F32)],
        scratch_shapes=scratch,
        compiler_params=pltpu.CompilerParams(
            dimension_semantics=("arbitrary", "arbitrary"),
            vmem_limit_bytes=VMEM_LIMIT_BYTES),
        name=f"{kind}_mixer_{'s' if has_s0 else 'p'}",
    )(*args)


def _ffn_kernel(*refs, bb, rows_t, has_s0):
    it = iter(refs)
    x_ref = next(it)
    c0_ref = next(it) if has_s0 else None
    w_up_ref = next(it)
    cw_ref = next(it)
    cb_ref = next(it)
    w_dn_ref = next(it)
    lng_ref = next(it)
    lnb_ref = next(it)
    y_ref = next(it)
    co_ref = next(it)
    a_ref = next(it)
    h_ref = next(it)
    acc_ref = next(it)

    rows = bb * rows_t
    t = pl.program_id(1)
    nt = pl.num_programs(1)
    pad = SUBLANES

    @pl.when(t == 0)
    def _init_carry():
        if has_s0:
            a_ref[:, pad - 2:pad, :] = c0_ref[...]
        else:
            a_ref[:, pad - 2:pad, :] = jnp.zeros((bb, 2, D_FF), F32)

    x = x_ref[...].reshape(rows, D_MODEL)
    xb = x.astype(BF16)

    def up_proj(n0):
        cs = slice(n0, n0 + FFN_COLS)
        a = _dot(xb, w_up_ref[:, cs])
        a_ref[:, pad:pad + rows_t, cs] = a.reshape(bb, rows_t, FFN_COLS)
        return _dot(xb, w_up_ref[:, D_FF + n0:D_FF + n0 + FFN_COLS])

    def activate(n0, g):
        cs = slice(n0, n0 + FFN_COLS)
        conv = (cb_ref[:, cs]
                + a_ref[:, pad - 2:pad - 2 + rows_t, cs] * cw_ref[0:1, cs]
                + a_ref[:, pad - 1:pad - 1 + rows_t, cs] * cw_ref[1:2, cs]
                + a_ref[:, pad:pad + rows_t, cs] * cw_ref[2:3, cs])
        h_ref[:, cs] = (_gelu_tanh(conv).reshape(rows, FFN_COLS) * g).astype(BF16)

    starts = list(range(0, D_FF, FFN_COLS))
    g_next = up_proj(starts[0])
    for idx, n0 in enumerate(starts):
        g = g_next
        if idx + 1 < len(starts):
            g_next = up_proj(starts[idx + 1])
        activate(n0, g)

    for n0 in range(0, D_MODEL, DOWN_COLS):
        acc_ref[:, n0:n0 + DOWN_COLS] = _dot(h_ref[...], w_dn_ref[:, n0:n0 + DOWN_COLS])

    out = _layer_norm(ALPHA * x + acc_ref[...], lng_ref[...], lnb_ref[...])
    y_ref[...] = out.reshape(y_ref.shape)

    last2 = a_ref[:, pad + rows_t - 2:pad + rows_t, :]
    a_ref[:, pad - 2:pad, :] = last2

    @pl.when(t == nt - 1)
    def _write_carry():
        co_ref[...] = last2


def _ffn_call(x, c0, weights, *, bb, rows_t):
    B, T, D = x.shape
    grid = (B // bb, T // rows_t)
    has_s0 = c0 is not None
    in_specs = [pl.BlockSpec((bb, rows_t, D), lambda b, t: (b, t, 0))]
    args = [x]
    if has_s0:
        in_specs.append(pl.BlockSpec((bb, CONV_WIDTH - 1, D_FF), lambda b, t: (b, 0, 0)))
        args.append(c0)
    for w in weights:
        in_specs.append(_const_spec(w.shape))
        args.append(w)
    kern = functools.partial(_ffn_kernel, bb=bb, rows_t=rows_t, has_s0=has_s0)
    return pl.pallas_call(
        kern,
        grid=grid,
        in_specs=in_specs,
        out_specs=[pl.BlockSpec((bb, rows_t, D), lambda b, t: (b, t, 0)),
                   pl.BlockSpec((bb, CONV_WIDTH - 1, D_FF), lambda b, t: (b, 0, 0))],
        out_shape=[jax.ShapeDtypeStruct((B, T, D), F32),
                   jax.ShapeDtypeStruct((B, CONV_WIDTH - 1, D_FF), F32)],
        scratch_shapes=[pltpu.VMEM((bb, SUBLANES + rows_t, D_FF), F32),
                        pltpu.VMEM((bb * rows_t, D_FF), BF16),
                        pltpu.VMEM((bb * rows_t, D), F32)],
        compiler_params=pltpu.CompilerParams(
            dimension_semantics=("arbitrary", "arbitrary"),
            vmem_limit_bytes=VMEM_LIMIT_BYTES),
        name=f"conv_ffn_{'s' if has_s0 else 'p'}",
    )(*args)


def _trunk(x, st_hgrn, st_gla, st_conv, layer_weights, *, mixer_bb, ffn_bb, rows_t, chunk):
    new_h, new_g, new_c = [], [], []
    for i in range(DEPTH):
        kind, mix_w, ffn_w = layer_weights[i]
        j = i // 2
        if kind == "hgrn":
            s0 = None if st_hgrn is None else st_hgrn[j]
        else:
            s0 = None if st_gla is None else st_gla[j]
        x, s = _mixer_call(kind, i, x, s0, mix_w, bb=mixer_bb, rows_t=rows_t, chunk=chunk)
        (new_h if kind == "hgrn" else new_g).append(s)
        c0 = None if st_conv is None else st_conv[i]
        x, c = _ffn_call(x, c0, ffn_w, bb=ffn_bb, rows_t=rows_t)
        new_c.append(c)
    return x, jnp.stack(new_h), jnp.stack(new_g), jnp.stack(new_c)


def kernel(x_prompt, x_sample, state_hgrn, state_gla, state_ffn_conv, lb_param, hgrn_w_in, hgrn_w_out, hgrn_norm_g, gla_w_in, gla_w_gk2, gla_b_gk2, gla_w_out, gla_norm_g, ln_mix_g, ln_mix_b, ffn_w_up, ffn_conv_w, ffn_conv_b, ffn_w_down, ln_ffn_g, ln_ffn_b):
    gla_main = 2 * GLA_KEY_DIM + 2 * GLA_VAL_DIM
    layer_weights = []
    for i in range(DEPTH):
        j = i // 2
        if i % 2 == 0:
            mix_w = (lb_param.astype(F32),
                     hgrn_w_in[j].astype(BF16),
                     hgrn_w_out[j].astype(BF16),
                     hgrn_norm_g[j].reshape(1, HGRN_DV),
                     ln_mix_g[i].reshape(1, D_MODEL),
                     ln_mix_b[i].reshape(1, D_MODEL))
            kind = "hgrn"
        else:
            w_r = jnp.pad(gla_w_in[j][:, gla_main:], ((0, 0), (0, LANES - GLA_GATE_RANK)))
            w_gk2 = jnp.pad(gla_w_gk2[j], ((0, LANES - GLA_GATE_RANK), (0, 0)))
            mix_w = (gla_w_in[j][:, :gla_main].astype(BF16),
                     w_r.astype(BF16),
                     w_gk2.astype(BF16),
                     gla_b_gk2[j].reshape(1, GLA_KEY_DIM),
                     gla_w_out[j].astype(BF16),
                     gla_norm_g[j].reshape(1, GLA_DV),
                     ln_mix_g[i].reshape(1, D_MODEL),
                     ln_mix_b[i].reshape(1, D_MODEL))
            kind = "gla"
        ffn_w = (ffn_w_up[i].astype(BF16),
                 ffn_conv_w[i],
                 ffn_conv_b[i].reshape(1, D_FF),
                 ffn_w_down[i].astype(BF16),
                 ln_ffn_g[i].reshape(1, D_MODEL),
                 ln_ffn_b[i].reshape(1, D_MODEL))
        layer_weights.append((kind, mix_w, ffn_w))

    y_p, h_p, g_p, c_p = _trunk(x_prompt, None, None, None, layer_weights,
                                mixer_bb=1, ffn_bb=1, rows_t=PROMPT_ROWS, chunk=CHUNK)
    dec_b, dec_t = x_sample.shape[0], x_sample.shape[1]
    y_s, h_s, g_s, c_s = _trunk(x_sample, state_hgrn, state_gla, state_ffn_conv, layer_weights,
                                mixer_bb=8, ffn_bb=dec_b, rows_t=dec_t, chunk=dec_t)
    return (y_p, y_s, h_p, h_s, g_p, g_s, c_p, c_s)
```

```python
import functools
import math

import jax
import jax.numpy as jnp
from jax import lax
from jax.experimental import pallas as pl
from jax.experimental.pallas import tpu as pltpu

F32 = jnp.float32
BF16 = jnp.bfloat16

D_MODEL = 1024
DEPTH = 2
CHUNK = 64
HGRN_HEADS = 8
HGRN_DK = 128
HGRN_DV = 128
HGRN_FDIM = HGRN_HEADS * HGRN_DK
GLA_HEADS = 4
GLA_KEY_DIM = 512
GLA_VAL_DIM = 1024
GLA_DK = 128
GLA_DV = 256
GLA_GATE_RANK = 16
GLA_GATE_NORMALIZER = 16.0
D_FF = 2816
CONV_WIDTH = 3
ALPHA = (2.0 * DEPTH) ** 0.25
LN_EPS = 1e-5
RMS_EPS = 1e-6
LOG2E = 1.4426950408889634

SUBLANES = 8
LANES = 128
VMEM_LIMIT_BYTES = 56 * 1024 * 1024
PROMPT_ROWS = 512
FFN_COLS = 256
DOWN_COLS = 512
PROJ_COLS = 512
MAX_FACTOR_LOG2 = 100.0
FAST_BLOCK_ROWS = 2 * SUBLANES
MODE_SINGLE, MODE_BLOCKS, MODE_PAIRWISE = 2, 1, 0
CHUNKS_PER_TRIP = 2


def _dot(a, b):
    return jnp.dot(a, b, preferred_element_type=F32)


def _dot_nt(a, b):
    return lax.dot_general(a, b, (((1,), (1,)), ((), ())), preferred_element_type=F32)


def _dot_tn(a, b):
    return lax.dot_general(a, b, (((0,), (0,)), ((), ())), preferred_element_type=F32)


def _sigmoid(x):
    return 1.0 / (1.0 + jnp.exp(-x))


def _silu(x):
    return x * _sigmoid(x)


def _gelu_tanh(x):
    c = math.sqrt(2.0 / math.pi)
    k = -2.0 * c * LOG2E
    z = x * ((k * 0.044715) * (x * x) + k)
    return x / (1.0 + jnp.exp2(z))


def _log2_sigmoid(x):
    return jnp.minimum(x, 0.0) * LOG2E - jnp.log2(1.0 + jnp.exp2(jnp.abs(x) * (-LOG2E)))


def _layer_norm(x, g, b):
    mu = jnp.mean(x, axis=-1, keepdims=True)
    xc = x - mu
    var = jnp.mean(xc * xc, axis=-1, keepdims=True)
    return xc * lax.rsqrt(var + LN_EPS) * g + b


def _cumsum_rows(g):
    c = g.shape[0]
    r = lax.broadcasted_iota(jnp.int32, (c, c), 0)
    s = lax.broadcasted_iota(jnp.int32, (c, c), 1)
    tri = (s <= r).astype(BF16)
    hi = g.astype(BF16)
    r1 = g - hi.astype(F32)
    mid = r1.astype(BF16)
    lo = (r1 - mid.astype(F32)).astype(BF16)
    return _dot(tri, hi) + _dot(tri, mid) + _dot(tri, lo)


def _mode_limits(chunk):
    half_windows = max(chunk // 2 // SUBLANES, 1)
    block_windows = min(FAST_BLOCK_ROWS, chunk) // SUBLANES
    return MAX_FACTOR_LOG2 / half_windows, MAX_FACTOR_LOG2 / block_windows


def _head_attention(q_view, k_view, b_view, bc_ref, kc_ref, r0, h, dk, v, chunk, mode):
    exact_diag = mode == MODE_PAIRWISE
    if mode == MODE_PAIRWISE:
        R = SUBLANES
    elif mode == MODE_BLOCKS:
        R = min(FAST_BLOCK_ROWS, chunk)
    else:
        R = chunk
    ref_in_block = R // 2 - 1 if mode == MODE_SINGLE else R - 1
    nb = chunk // R
    static_rows = isinstance(r0, int)

    def cols(view):
        return slice(view[1] + h * dk, view[1] + (h + 1) * dk)

    def blk(view, i):
        if static_rows:
            return view[0][r0 + R * i:r0 + R * (i + 1), cols(view)]
        return view[0][pl.ds(pl.multiple_of(r0 + R * i, R), R), cols(view)]

    def row(view, copy_ref, t):
        if static_rows:
            one = view[0][r0 + t:r0 + t + 1, cols(view)]
        else:
            one = copy_ref[t:t + 1, h * dk:(h + 1) * dk]
        return jnp.broadcast_to(one, (R, dk))

    def full(view):
        if static_rows:
            return view[0][r0:r0 + chunk, cols(view)]
        return view[0][pl.ds(r0, chunk), cols(view)]

    def b_end(j):
        return row(b_view, bc_ref, R * j + ref_in_block)

    kpp = jnp.concatenate(
        [blk(k_view, j) * jnp.exp2(b_end(j) - blk(b_view, j)) for j in range(nb)],
        axis=0).astype(BF16)

    col = lax.broadcasted_iota(jnp.int32, (R, chunk), 1)
    rw = lax.broadcasted_iota(jnp.int32, (R, chunk), 0)
    colblk = col // R

    lq, pair = [], {}
    for i in range(nb):
        qi, bi = blk(q_view, i), blk(b_view, i)
        for j in range(i if exact_diag else i + 1):
            pair[(i, j)] = len(lq)
            lq.append(qi * jnp.exp2(bi - b_end(j)))
    if lq:
        cross = _dot_nt(jnp.concatenate(lq, axis=0).astype(BF16), kpp)

    def cross_blk(i, j):
        p = pair[(i, j)]
        return cross[R * p:R * (p + 1), :]

    b_all = full(b_view)
    if static_rows:
        b_last = b_view[0][r0 + chunk - 1:r0 + chunk, cols(b_view)]
    else:
        b_last = bc_ref[chunk - 1:chunk, h * dk:(h + 1) * dk]
    vb = v.astype(BF16)
    qe = (full(q_view) * jnp.exp2(b_all)).astype(BF16)
    kd = (full(k_view) * jnp.exp2(b_last - b_all)).astype(BF16)
    kv = _dot_tn(vb, kd)

    def finish(st):
        a_rows = []
        for i in range(nb):
            if exact_diag:
                acc = jnp.zeros((R, chunk), F32)
                for j in range(i):
                    acc = jnp.where(colblk == j, cross_blk(i, j), acc)
                qi, bi = blk(q_view, i), blk(b_view, i)
                for s in range(R):
                    t = R * i + s
                    e = jnp.exp2(jnp.minimum(bi - row(b_view, bc_ref, t), 0.0))
                    cs = jnp.sum(qi * e * row(k_view, kc_ref, t), axis=-1, keepdims=True)
                    acc = jnp.where(col == t, cs, acc)
            else:
                acc = cross_blk(i, 0)
                for j in range(1, i + 1):
                    acc = jnp.where(colblk == j, cross_blk(i, j), acc)
            a_rows.append(jnp.where(col <= R * i + rw, acc, 0.0))
        a = jnp.concatenate(a_rows, axis=0).astype(BF16)
        o = _dot(a, vb) + _dot_nt(qe, st.astype(BF16))
        st_new = jnp.exp2(b_last) * st + kv
        return o, st_new

    return finish


def _rms_gate(o, norm_g, gate_act):
    ms = jnp.mean(o * o, axis=-1, keepdims=True)
    return o * lax.rsqrt(ms + RMS_EPS) * norm_g * gate_act


def _mixer_kernel(*refs, kind, layer, bb, nch, chunk, has_s0):
    it = iter(refs)
    x_ref = next(it)
    xp_ref = next(it)
    s0_ref = next(it) if has_s0 else None
    if kind == "hgrn":
        lb_ref = next(it)
        w_in_ref = next(it)
    else:
        w_in_ref = next(it)
        w_r_ref = next(it)
        w_gk2_ref = next(it)
        b_gk2_ref = next(it)
    w_out_ref = next(it)
    ng_ref = next(it)
    lng_ref = next(it)
    lnb_ref = next(it)
    y_ref = next(it)
    so_ref = next(it)
    proj2_ref = next(it)
    aux2_ref = next(it)
    bc_ref = next(it)
    kc_ref = next(it)
    o_ref = next(it)
    st_ref = next(it)
    flag_ref = next(it)

    heads, dk, dv = (HGRN_HEADS, HGRN_DK, HGRN_DV) if kind == "hgrn" else (GLA_HEADS, GLA_DK, GLA_DV)
    kdim = heads * dk
    rows = bb * nch * chunk
    n_chunks = bb * nch
    nb = chunk // SUBLANES
    ncols = proj2_ref.shape[2]
    s = pl.program_id(1)
    ns = pl.num_programs(1)
    cur = s % 2
    prv = 1 - cur
    has_cur = s < ns - 1
    has_prev = s > 0

    if kind == "hgrn":
        q_off, f_off, v_off, g_off = 0, kdim, 2 * kdim, 3 * kdim
    else:
        q_off, k_off, v_off, g_off = 0, kdim, 2 * kdim, 2 * kdim + heads * dv

    def views(slot):
        proj, aux = proj2_ref.at[slot], aux2_ref.at[slot]
        if kind == "hgrn":
            return proj, aux, (proj, q_off), (aux, 0), (proj, f_off)
        return proj, aux, (proj, q_off), (proj, k_off), (aux, 0)

    @pl.when(s == 0)
    def _init():
        flag_ref[0] = 0
        flag_ref[1] = 0
        for b in range(bb):
            for h in range(heads):
                if has_s0:
                    st_ref[b, h] = s0_ref[b, h].T
                else:
                    st_ref[b, h] = jnp.zeros((dv, dk), F32)

    ng = ng_ref[...]

    def projection_items(slot):
        proj, aux, _, _, _ = views(slot)
        xb = x_ref[...].reshape(rows, D_MODEL).astype(BF16)
        decay_max = []

        def cumsum_in_place(ref, cs):
            m = jnp.zeros((SUBLANES, cs.stop - cs.start), F32)
            for c in range(n_chunks):
                rs = slice(c * chunk, (c + 1) * chunk)
                b = _cumsum_rows(ref[rs, cs])
                ref[rs, cs] = b
                m = jnp.maximum(m, -b[0:SUBLANES])
                for j in range(1, nb):
                    m = jnp.maximum(m, b[SUBLANES * (j - 1):SUBLANES * j] - b[SUBLANES * j:SUBLANES * (j + 1)])
            decay_max.append(jnp.max(m))

        if kind == "hgrn":
            p = lb_ref[...]
            pe = jnp.exp(p - jnp.max(p, axis=0, keepdims=True))
            lb = jnp.sum(pe[:layer + 1], axis=0, keepdims=True) / jnp.sum(pe, axis=0, keepdims=True)

        def col_block(n0):
            cs = slice(n0, n0 + PROJ_COLS)
            blk = _dot(xb, w_in_ref[:, cs])
            if n0 >= g_off:
                proj[:, cs] = _silu(blk)
            elif n0 >= v_off:
                proj[:, cs] = blk
            elif kind == "hgrn" and n0 >= f_off:
                fs = slice(n0 - f_off, n0 - f_off + PROJ_COLS)
                f = lb[:, fs] + (1.0 - lb[:, fs]) * _sigmoid(blk)
                aux[:, fs] = 1.0 - f
                proj[:, cs] = jnp.log2(f)
                cumsum_in_place(proj, cs)
            elif kind == "hgrn":
                proj[:, cs] = _silu(blk)
            elif n0 >= k_off:
                proj[:, cs] = blk
            else:
                proj[:, cs] = blk * (dk ** -0.5)

        def gla_gate():
            r = _dot(xb, w_r_ref[...]).astype(BF16)
            z = _dot(r, w_gk2_ref[...]) + b_gk2_ref[...]
            aux[...] = _log2_sigmoid(z) * (1.0 / GLA_GATE_NORMALIZER)
            cumsum_in_place(aux, slice(0, kdim))

        def record_bound():
            worst = functools.reduce(jnp.maximum, decay_max)
            single_max, blocks_max = _mode_limits(chunk)
            flag_ref[slot] = jnp.where(
                worst <= single_max, MODE_SINGLE,
                jnp.where(worst <= blocks_max, MODE_BLOCKS, MODE_PAIRWISE)).astype(jnp.int32)

        items = [functools.partial(col_block, n0) for n0 in range(0, ncols, PROJ_COLS)]
        if kind == "gla":
            items.append(gla_gate)
        return items, record_bound

    def finish_chunk(proj, c_rows, b_idx, finishers):
        for h, finish in enumerate(finishers):
            o, st_new = finish(st_ref[b_idx, h])
            st_ref[b_idx, h] = st_new
            gate = proj[c_rows, g_off + h * dv:g_off + (h + 1) * dv]
            o_ref[c_rows, h * dv:(h + 1) * dv] = _rms_gate(o, ng, gate).astype(BF16)

    def attention_items(slot):
        proj, _, q_view, k_view, b_view = views(slot)
        items, staged = [], {}

        def issue(chunks):
            for c in chunks:
                r0 = c * chunk
                staged[c] = [_head_attention(q_view, k_view, b_view, None, None, r0, h, dk,
                                             proj[r0:r0 + chunk, v_off + h * dv:v_off + (h + 1) * dv],
                                             chunk, MODE_SINGLE) for h in range(heads)]

        def consume(chunks):
            for c in chunks:
                finish_chunk(proj, slice(c * chunk, (c + 1) * chunk), c // nch, staged.pop(c))

        for c0 in range(0, n_chunks, CHUNKS_PER_TRIP):
            chunks = list(range(c0, c0 + CHUNKS_PER_TRIP))
            items += [functools.partial(issue, chunks), functools.partial(consume, chunks)]
        return items

    def attention_loop(slot, mode):
        exact_diag = mode == MODE_PAIRWISE
        proj, _, q_view, k_view, b_view = views(slot)

        def first_stage(ci, copy):
            r0 = pl.multiple_of(ci * chunk, chunk)
            rs = pl.ds(r0, chunk)
            bc, kc = bc_ref.at[copy], kc_ref.at[copy]
            bc[...] = b_view[0][rs, b_view[1]:b_view[1] + kdim]
            if exact_diag:
                kc[...] = k_view[0][rs, k_view[1]:k_view[1] + kdim]
            return [_head_attention(q_view, k_view, b_view, bc, kc, r0, h, dk,
                                    proj[rs, v_off + h * dv:v_off + (h + 1) * dv],
                                    chunk, mode) for h in range(heads)]

        per_trip = 1 if exact_diag else CHUNKS_PER_TRIP

        def trip(ti, carry):
            staged = [first_stage(ti * per_trip + j, j) for j in range(per_trip)]
            for j in range(per_trip):
                ci = ti * per_trip + j
                finish_chunk(proj, pl.ds(pl.multiple_of(ci * chunk, chunk), chunk), ci // nch, staged[j])
            return carry
        lax.fori_loop(0, n_chunks // per_trip, trip, 0)

    prev_mode = flag_ref[prv]
    overlap = has_cur & has_prev & (prev_mode == MODE_SINGLE)

    def project_and_attend(slot):
        proj_items, record_bound = projection_items(slot)
        attn_items = attention_items(1 - slot)
        done = 0
        for i, attend in enumerate(attn_items):
            upto = -(-len(proj_items) * (i + 1) // len(attn_items))
            for item in proj_items[done:upto]:
                item()
            done = upto
            attend()
        record_bound()

    for slot in range(2):
        pl.when(overlap & (cur == slot))(functools.partial(project_and_attend, slot))

    @pl.when(jnp.logical_not(overlap))
    def _one_at_a_time():
        for mode in (MODE_SINGLE, MODE_BLOCKS, MODE_PAIRWISE):
            pl.when(has_prev & (prev_mode == mode))(functools.partial(attention_loop, prv, mode))

        @pl.when(has_cur)
        def _project():
            proj_items, record_bound = projection_items(cur)
            for item in proj_items:
                item()
            record_bound()

    @pl.when(has_prev)
    def _output():
        xp = xp_ref[...].reshape(rows, D_MODEL)
        y = _dot(o_ref[...], w_out_ref[...])
        out = _layer_norm(ALPHA * xp + y, lng_ref[...], lnb_ref[...])
        y_ref[...] = out.reshape(y_ref.shape)

    @pl.when(s == ns - 1)
    def _write_state():
        for b in range(bb):
            for h in range(heads):
                so_ref[b, h] = st_ref[b, h].T


def _const_spec(shape):
    nd = len(shape)
    return pl.BlockSpec(shape, lambda b, s: (0,) * nd, pipeline_mode=pl.Buffered(1))


def _mixer_call(kind, layer, x, s0, weights, *, bb, rows_t, chunk):
    B, T, D = x.shape
    heads, dk, dv = (HGRN_HEADS, HGRN_DK, HGRN_DV) if kind == "hgrn" else (GLA_HEADS, GLA_DK, GLA_DV)
    nch = rows_t // chunk
    rows = bb * rows_t
    nt = T // rows_t
    grid = (B // bb, nt + 1)
    has_s0 = s0 is not None

    def this_block(b, s):
        return (b, jnp.minimum(s, nt - 1), 0)

    def prev_block(b, s):
        return (b, jnp.maximum(s - 1, 0), 0)

    in_specs = [pl.BlockSpec((bb, rows_t, D), this_block),
                pl.BlockSpec((bb, rows_t, D), prev_block)]
    args = [x, x]
    if has_s0:
        in_specs.append(pl.BlockSpec((bb, heads, dk, dv), lambda b, s: (b, 0, 0, 0)))
        args.append(s0)
    for w in weights:
        in_specs.append(_const_spec(w.shape))
        args.append(w)

    ncols = weights[1].shape[1] if kind == "hgrn" else weights[0].shape[1]
    scratch = [
        pltpu.VMEM((2, rows, ncols), F32),
        pltpu.VMEM((2, rows, heads * dk), F32),
        pltpu.VMEM((CHUNKS_PER_TRIP, chunk, heads * dk), F32),
        pltpu.VMEM((CHUNKS_PER_TRIP, chunk, heads * dk), F32),
        pltpu.VMEM((rows, heads * dv), BF16),
        pltpu.VMEM((bb, heads, dv, dk), F32),
        pltpu.SMEM((2,), jnp.int32),
    ]
    kern = functools.partial(_mixer_kernel, kind=kind, layer=layer, bb=bb, nch=nch, chunk=chunk,
                             has_s0=has_s0)
    return pl.pallas_call(
        kern,
        grid=grid,
        in_specs=in_specs,
        out_specs=[pl.BlockSpec((bb, rows_t, D), prev_block),
                   pl.BlockSpec((bb, heads, dk, dv), lambda b, s: (b, 0, 0, 0))],
        out_shape=[jax.ShapeDtypeStruct((B, T, D), F32),
                   jax.ShapeDtypeStruct((B, heads, dk, dv), F32)],
        scratch_shapes=scratch,
        compiler_params=pltpu.CompilerParams(
            dimension_semantics=("arbitrary", "arbitrary"),
            vmem_limit_bytes=VMEM_LIMIT_BYTES),
        name=f"{kind}_mixer_{'s' if has_s0 else 'p'}",
    )(*args)


def _ffn_kernel(*refs, bb, rows_t, has_s0):
    it = iter(refs)
    x_ref = next(it)
    c0_ref = next(it) if has_s0 else None
    w_up_ref = next(it)
    cw_ref = next(it)
    cb_ref = next(it)
    w_dn_ref = next(it)
    lng_ref = next(it)
    lnb_ref = next(it)
    y_ref = next(it)
    co_ref = next(it)
    a_ref = next(it)
    h_ref = next(it)
    acc_ref = next(it)

    rows = bb * rows_t
    t = pl.program_id(1)
    nt = pl.num_programs(1)
    pad = SUBLANES

    @pl.when(t == 0)
    def _init_carry():
        if has_s0:
            a_ref[:, pad - 2:pad, :] = c0_ref[...]
        else:
            a_ref[:, pad - 2:pad, :] = jnp.zeros((bb, 2, D_FF), F32)

    x = x_ref[...].reshape(rows, D_MODEL)
    xb = x.astype(BF16)

    def up_proj(n0):
        cs = slice(n0, n0 + FFN_COLS)
        a = _dot(xb, w_up_ref[:, cs])
        a_ref[:, pad:pad + rows_t, cs] = a.reshape(bb, rows_t, FFN_COLS)
        return _dot(xb, w_up_ref[:, D_FF + n0:D_FF + n0 + FFN_COLS])

    def activate(n0, g):
        cs = slice(n0, n0 + FFN_COLS)
        conv = (cb_ref[:, cs]
                + a_ref[:, pad - 2:pad - 2 + rows_t, cs] * cw_ref[0:1, cs]
                + a_ref[:, pad - 1:pad - 1 + rows_t, cs] * cw_ref[1:2, cs]
                + a_ref[:, pad:pad + rows_t, cs] * cw_ref[2:3, cs])
        h_ref[:, cs] = (_gelu_tanh(conv).reshape(rows, FFN_COLS) * g).astype(BF16)

    starts = list(range(0, D_FF, FFN_COLS))
    g_next = up_proj(starts[0])
    for idx, n0 in enumerate(starts):
        g = g_next
        if idx + 1 < len(starts):
            g_next = up_proj(starts[idx + 1])
        activate(n0, g)

    for n0 in range(0, D_MODEL, DOWN_COLS):
        acc_ref[:, n0:n0 + DOWN_COLS] = _dot(h_ref[...], w_dn_ref[:, n0:n0 + DOWN_COLS])

    out = _layer_norm(ALPHA * x + acc_ref[...], lng_ref[...], lnb_ref[...])
    y_ref[...] = out.reshape(y_ref.shape)

    last2 = a_ref[:, pad + rows_t - 2:pad + rows_t, :]
    a_ref[:, pad - 2:pad, :] = last2

    @pl.when(t == nt - 1)
    def _write_carry():
        co_ref[...] = last2


def _ffn_call(x, c0, weights, *, bb, rows_t):
    B, T, D = x.shape
    grid = (B // bb, T // rows_t)
    has_s0 = c0 is not None
    in_specs = [pl.BlockSpec((bb, rows_t, D), lambda b, t: (b, t, 0))]
    args = [x]
    if has_s0:
        in_specs.append(pl.BlockSpec((bb, CONV_WIDTH - 1, D_FF), lambda b, t: (b, 0, 0)))
        args.append(c0)
    for w in weights:
        in_specs.append(_const_spec(w.shape))
        args.append(w)
    kern = functools.partial(_ffn_kernel, bb=bb, rows_t=rows_t, has_s0=has_s0)
    return pl.pallas_call(
        kern,
        grid=grid,
        in_specs=in_specs,
        out_specs=[pl.BlockSpec((bb, rows_t, D), lambda b, t: (b, t, 0)),
                   pl.BlockSpec((bb, CONV_WIDTH - 1, D_FF), lambda b, t: (b, 0, 0))],
        out_shape=[jax.ShapeDtypeStruct((B, T, D), F32),
                   jax.ShapeDtypeStruct((B, CONV_WIDTH - 1, D_FF), F32)],
        scratch_shapes=[pltpu.VMEM((bb, SUBLANES + rows_t, D_FF), F32),
                        pltpu.VMEM((bb * rows_t, D_FF), BF16),
                        pltpu.VMEM((bb * rows_t, D), F32)],
        compiler_params=pltpu.CompilerParams(
            dimension_semantics=("arbitrary", "arbitrary"),
            vmem_limit_bytes=VMEM_LIMIT_BYTES),
        name=f"conv_ffn_{'s' if has_s0 else 'p'}",
    )(*args)


def _trunk(x, st_hgrn, st_gla, st_conv, layer_weights, *, mixer_bb, ffn_bb, rows_t, chunk):
    new_h, new_g, new_c = [], [], []
    for i in range(DEPTH):
        kind, mix_w, ffn_w = layer_weights[i]
        j = i // 2
        if kind == "hgrn":
            s0 = None if st_hgrn is None else st_hgrn[j]
        else:
            s0 = None if st_gla is None else st_gla[j]
        x, s = _mixer_call(kind, i, x, s0, mix_w, bb=mixer_bb, rows_t=rows_t, chunk=chunk)
        (new_h if kind == "hgrn" else new_g).append(s)
        c0 = None if st_conv is None else st_conv[i]
        x, c = _ffn_call(x, c0, ffn_w, bb=ffn_bb, rows_t=rows_t)
        new_c.append(c)
    return x, jnp.stack(new_h), jnp.stack(new_g), jnp.stack(new_c)


def kernel(x_prompt, x_sample, state_hgrn, state_gla, state_ffn_conv, lb_param, hgrn_w_in, hgrn_w_out, hgrn_norm_g, gla_w_in, gla_w_gk2, gla_b_gk2, gla_w_out, gla_norm_g, ln_mix_g, ln_mix_b, ffn_w_up, ffn_conv_w, ffn_conv_b, ffn_w_down, ln_ffn_g, ln_ffn_b):
    gla_main = 2 * GLA_KEY_DIM + 2 * GLA_VAL_DIM
    layer_weights = []
    for i in range(DEPTH):
        j = i // 2
        if i % 2 == 0:
            mix_w = (lb_param.astype(F32),
                     hgrn_w_in[j].astype(BF16),
                     hgrn_w_out[j].astype(BF16),
                     hgrn_norm_g[j].reshape(1, HGRN_DV),
                     ln_mix_g[i].reshape(1, D_MODEL),
                     ln_mix_b[i].reshape(1, D_MODEL))
            kind = "hgrn"
        else:
            w_r = jnp.pad(gla_w_in[j][:, gla_main:], ((0, 0), (0, LANES - GLA_GATE_RANK)))
            w_gk2 = jnp.pad(gla_w_gk2[j], ((0, LANES - GLA_GATE_RANK), (0, 0)))
            mix_w = (gla_w_in[j][:, :gla_main].astype(BF16),
                     w_r.astype(BF16),
                     w_gk2.astype(BF16),
                     gla_b_gk2[j].reshape(1, GLA_KEY_DIM),
                     gla_w_out[j].astype(BF16),
                     gla_norm_g[j].reshape(1, GLA_DV),
                     ln_mix_g[i].reshape(1, D_MODEL),
                     ln_mix_b[i].reshape(1, D_MODEL))
            kind = "gla"
        ffn_w = (ffn_w_up[i].astype(BF16),
                 ffn_conv_w[i],
                 ffn_conv_b[i].reshape(1, D_FF),
                 ffn_w_down[i].astype(BF16),
                 ln_ffn_g[i].reshape(1, D_MODEL),
                 ln_ffn_b[i].reshape(1, D_MODEL))
        layer_weights.append((kind, mix_w, ffn_w))

    y_p, h_p, g_p, c_p = _trunk(x_prompt, None, None, None, layer_weights,
                                mixer_bb=1, ffn_bb=1, rows_t=PROMPT_ROWS, chunk=CHUNK)
    dec_b, dec_t = x_sample.shape[0], x_sample.shape[1]
    y_s, h_s, g_s, c_s = _trunk(x_sample, state_hgrn, state_gla, state_ffn_conv, layer_weights,
                                mixer_bb=8, ffn_bb=dec_b, rows_t=dec_t, chunk=dec_t)
    return (y_p, y_s, h_p, h_s, g_p, g_s, c_p, c_s)
```

```python
import functools
import math

import jax
import jax.numpy as jnp
from jax import lax
from jax.experimental import pallas as pl
from jax.experimental.pallas import tpu as pltpu

F32 = jnp.float32
BF16 = jnp.bfloat16

D_MODEL = 1024
DEPTH = 2
CHUNK = 64
HGRN_HEADS = 8
HGRN_DK = 128
HGRN_DV = 128
HGRN_FDIM = HGRN_HEADS * HGRN_DK
GLA_HEADS = 4
GLA_KEY_DIM = 512
GLA_VAL_DIM = 1024
GLA_DK = 128
GLA_DV = 256
GLA_GATE_RANK = 16
GLA_GATE_NORMALIZER = 16.0
D_FF = 2816
CONV_WIDTH = 3
ALPHA = (2.0 * DEPTH) ** 0.25
LN_EPS = 1e-5
RMS_EPS = 1e-6
LOG2E = 1.4426950408889634

SUBLANES = 8
LANES = 128
VMEM_LIMIT_BYTES = 56 * 1024 * 1024
PROMPT_ROWS = 512
FFN_COLS = 256
DOWN_COLS = 512
PROJ_COLS = 512
MAX_FACTOR_LOG2 = 100.0
FAST_BLOCK_ROWS = 2 * SUBLANES
MODE_SINGLE, MODE_BLOCKS, MODE_PAIRWISE = 2, 1, 0
CHUNKS_PER_TRIP = 2


def _dot(a, b):
    return jnp.dot(a, b, preferred_element_type=F32)


def _dot_nt(a, b):
    return lax.dot_general(a, b, (((1,), (1,)), ((), ())), preferred_element_type=F32)


def _dot_tn(a, b):
    return lax.dot_general(a, b, (((0,), (0,)), ((), ())), preferred_element_type=F32)


def _sigmoid(x):
    return 1.0 / (1.0 + jnp.exp(-x))


def _silu(x):
    return x * _sigmoid(x)


def _gelu_tanh(x):
    c = math.sqrt(2.0 / math.pi)
    k = -2.0 * c * LOG2E
    z = x * ((k * 0.044715) * (x * x) + k)
    return x / (1.0 + jnp.exp2(z))


def _log2_sigmoid(x):
    return jnp.minimum(x, 0.0) * LOG2E - jnp.log2(1.0 + jnp.exp2(jnp.abs(x) * (-LOG2E)))


def _layer_norm(x, g, b):
    mu = jnp.mean(x, axis=-1, keepdims=True)
    xc = x - mu
    var = jnp.mean(xc * xc, axis=-1, keepdims=True)
    return xc * lax.rsqrt(var + LN_EPS) * g + b


def _cumsum_rows(g):
    c = g.shape[0]
    r = lax.broadcasted_iota(jnp.int32, (c, c), 0)
    s = lax.broadcasted_iota(jnp.int32, (c, c), 1)
    tri = (s <= r).astype(BF16)
    hi = g.astype(BF16)
    r1 = g - hi.astype(F32)
    mid = r1.astype(BF16)
    lo = (r1 - mid.astype(F32)).astype(BF16)
    return _dot(tri, hi) + _dot(tri, mid) + _dot(tri, lo)


def _mode_limits(chunk):
    half_windows = max(chunk // 2 // SUBLANES, 1)
    block_windows = min(FAST_BLOCK_ROWS, chunk) // SUBLANES
    return MAX_FACTOR_LOG2 / half_windows, MAX_FACTOR_LOG2 / block_windows


def _head_attention(q_view, k_view, b_view, bc_ref, kc_ref, r0, h, dk, v, chunk, mode):
    exact_diag = mode == MODE_PAIRWISE
    if mode == MODE_PAIRWISE:
        R = SUBLANES
    elif mode == MODE_BLOCKS:
        R = min(FAST_BLOCK_ROWS, chunk)
    else:
        R = chunk
    ref_in_block = R // 2 - 1 if mode == MODE_SINGLE else R - 1
    nb = chunk // R
    static_rows = isinstance(r0, int)

    def cols(view):
        return slice(view[1] + h * dk, view[1] + (h + 1) * dk)

    def blk(view, i):
        if static_rows:
            return view[0][r0 + R * i:r0 + R * (i + 1), cols(view)]
        return view[0][pl.ds(pl.multiple_of(r0 + R * i, R), R), cols(view)]

    def row(view, copy_ref, t):
        if static_rows:
            one = view[0][r0 + t:r0 + t + 1, cols(view)]
        else:
            one = copy_ref[t:t + 1, h * dk:(h + 1) * dk]
        return jnp.broadcast_to(one, (R, dk))

    def full(view):
        if static_rows:
            return view[0][r0:r0 + chunk, cols(view)]
        return view[0][pl.ds(r0, chunk), cols(view)]

    def b_end(j):
        return row(b_view, bc_ref, R * j + ref_in_block)

    kpp = jnp.concatenate(
        [blk(k_view, j) * jnp.exp2(b_end(j) - blk(b_view, j)) for j in range(nb)],
        axis=0).astype(BF16)

    col = lax.broadcasted_iota(jnp.int32, (R, chunk), 1)
    rw = lax.broadcasted_iota(jnp.int32, (R, chunk), 0)
    colblk = col // R

    lq, pair = [], {}
    for i in range(nb):
        qi, bi = blk(q_view, i), blk(b_view, i)
        for j in range(i if exact_diag else i + 1):
            pair[(i, j)] = len(lq)
            lq.append(qi * jnp.exp2(bi - b_end(j)))
    if lq:
        cross = _dot_nt(jnp.concatenate(lq, axis=0).astype(BF16), kpp)

    def cross_blk(i, j):
        p = pair[(i, j)]
        return cross[R * p:R * (p + 1), :]

    b_all = full(b_view)
    if static_rows:
        b_last = b_view[0][r0 + chunk - 1:r0 + chunk, cols(b_view)]
    else:
        b_last = bc_ref[chunk - 1:chunk, h * dk:(h + 1) * dk]
    vb = v.astype(BF16)
    qe = (full(q_view) * jnp.exp2(b_all)).astype(BF16)
    kd = (full(k_view) * jnp.exp2(b_last - b_all)).astype(BF16)
    kv = _dot_tn(kd, vb)
    dv = v.shape[1]
    decay = jnp.broadcast_to(jnp.exp2(b_last), (dk, dk)).T
    if dv > dk:
        decay = jnp.concatenate([decay] * (dv // dk), axis=1)

    def finish(st):
        a_rows = []
        for i in range(nb):
            if exact_diag:
                acc = jnp.zeros((R, chunk), F32)
                for j in range(i):
                    acc = jnp.where(colblk == j, cross_blk(i, j), acc)
                qi, bi = blk(q_view, i), blk(b_view, i)
                for s in range(R):
                    t = R * i + s
                    e = jnp.exp2(jnp.minimum(bi - row(b_view, bc_ref, t), 0.0))
                    cs = jnp.sum(qi * e * row(k_view, kc_ref, t), axis=-1, keepdims=True)
                    acc = jnp.where(col == t, cs, acc)
            else:
                acc = cross_blk(i, 0)
                for j in range(1, i + 1):
                    acc = jnp.where(colblk == j, cross_blk(i, j), acc)
            a_rows.append(jnp.where(col <= R * i + rw, acc, 0.0))
        a = jnp.concatenate(a_rows, axis=0).astype(BF16)
        o = _dot(jnp.concatenate([qe, a], axis=1),
                 jnp.concatenate([st.astype(BF16), vb], axis=0))
        st_new = decay * st + kv
        return o, st_new

    return finish


def _rms_gate(o, norm_g, gate_act):
    ms = jnp.mean(o * o, axis=-1, keepdims=True)
    return o * lax.rsqrt(ms + RMS_EPS) * norm_g * gate_act


def _mixer_kernel(*refs, kind, layer, bb, nch, chunk, has_s0):
    it = iter(refs)
    x_ref = next(it)
    xp_ref = next(it)
    s0_ref = next(it) if has_s0 else None
    if kind == "hgrn":
        lb_ref = next(it)
        w_in_ref = next(it)
    else:
        w_in_ref = next(it)
        w_r_ref = next(it)
        w_gk2_ref = next(it)
        b_gk2_ref = next(it)
    w_out_ref = next(it)
    ng_ref = next(it)
    lng_ref = next(it)
    lnb_ref = next(it)
    y_ref = next(it)
    so_ref = next(it)
    proj2_ref = next(it)
    aux2_ref = next(it)
    bc_ref = next(it)
    kc_ref = next(it)
    o_ref = next(it)
    st_ref = next(it)
    flag_ref = next(it)

    heads, dk, dv = (HGRN_HEADS, HGRN_DK, HGRN_DV) if kind == "hgrn" else (GLA_HEADS, GLA_DK, GLA_DV)
    kdim = heads * dk
    rows = bb * nch * chunk
    n_chunks = bb * nch
    nb = chunk // SUBLANES
    ncols = proj2_ref.shape[2]
    s = pl.program_id(1)
    ns = pl.num_programs(1)
    cur = s % 2
    prv = 1 - cur
    has_cur = s < ns - 1
    has_prev = s > 0

    if kind == "hgrn":
        q_off, f_off, v_off, g_off = 0, kdim, 2 * kdim, 3 * kdim
    else:
        q_off, k_off, v_off, g_off = 0, kdim, 2 * kdim, 2 * kdim + heads * dv

    def views(slot):
        proj, aux = proj2_ref.at[slot], aux2_ref.at[slot]
        if kind == "hgrn":
            return proj, aux, (proj, q_off), (aux, 0), (proj, f_off)
        return proj, aux, (proj, q_off), (proj, k_off), (aux, 0)

    @pl.when(s == 0)
    def _init():
        flag_ref[0] = 0
        flag_ref[1] = 0
        if has_s0:
            st_ref[...] = s0_ref[...]
        else:
            st_ref[...] = jnp.zeros(st_ref.shape, F32)

    ng = ng_ref[...]

    def projection_items(slot):
        proj, aux, _, _, _ = views(slot)
        xb = x_ref[...].reshape(rows, D_MODEL).astype(BF16)
        decay_max = []

        def cumsum_in_place(ref, cs):
            m = jnp.zeros((SUBLANES, cs.stop - cs.start), F32)
            for c in range(n_chunks):
                rs = slice(c * chunk, (c + 1) * chunk)
                b = _cumsum_rows(ref[rs, cs])
                ref[rs, cs] = b
                m = jnp.maximum(m, -b[0:SUBLANES])
                for j in range(1, nb):
                    m = jnp.maximum(m, b[SUBLANES * (j - 1):SUBLANES * j] - b[SUBLANES * j:SUBLANES * (j + 1)])
            decay_max.append(jnp.max(m))

        if kind == "hgrn":
            p = lb_ref[...]
            pe = jnp.exp(p - jnp.max(p, axis=0, keepdims=True))
            lb = jnp.sum(pe[:layer + 1], axis=0, keepdims=True) / jnp.sum(pe, axis=0, keepdims=True)

        def col_block(n0):
            cs = slice(n0, n0 + PROJ_COLS)
            blk = _dot(xb, w_in_ref[:, cs])
            if n0 >= g_off:
                proj[:, cs] = _silu(blk)
            elif n0 >= v_off:
                proj[:, cs] = blk
            elif kind == "hgrn" and n0 >= f_off:
                fs = slice(n0 - f_off, n0 - f_off + PROJ_COLS)
                f = lb[:, fs] + (1.0 - lb[:, fs]) * _sigmoid(blk)
                aux[:, fs] = 1.0 - f
                proj[:, cs] = jnp.log2(f)
                cumsum_in_place(proj, cs)
            elif kind == "hgrn":
                proj[:, cs] = _silu(blk)
            elif n0 >= k_off:
                proj[:, cs] = blk
            else:
                proj[:, cs] = blk * (dk ** -0.5)

        def gla_gate():
            r = _dot(xb, w_r_ref[...]).astype(BF16)
            z = _dot(r, w_gk2_ref[...]) + b_gk2_ref[...]
            aux[...] = _log2_sigmoid(z) * (1.0 / GLA_GATE_NORMALIZER)
            cumsum_in_place(aux, slice(0, kdim))

        def record_bound():
            worst = functools.reduce(jnp.maximum, decay_max)
            single_max, blocks_max = _mode_limits(chunk)
            flag_ref[slot] = jnp.where(
                worst <= single_max, MODE_SINGLE,
                jnp.where(worst <= blocks_max, MODE_BLOCKS, MODE_PAIRWISE)).astype(jnp.int32)

        items = [functools.partial(col_block, n0) for n0 in range(0, ncols, PROJ_COLS)]
        if kind == "gla":
            items.append(gla_gate)
        return items, record_bound

    def finish_chunk(proj, c_rows, b_idx, finishers):
        for h, finish in enumerate(finishers):
            o, st_new = finish(st_ref[b_idx, h])
            st_ref[b_idx, h] = st_new
            gate = proj[c_rows, g_off + h * dv:g_off + (h + 1) * dv]
            o_ref[c_rows, h * dv:(h + 1) * dv] = _rms_gate(o, ng, gate).astype(BF16)

    def attention_items(slot):
        proj, _, q_view, k_view, b_view = views(slot)
        items, staged = [], {}

        def issue(chunks):
            for c in chunks:
                r0 = c * chunk
                staged[c] = [_head_attention(q_view, k_view, b_view, None, None, r0, h, dk,
                                             proj[r0:r0 + chunk, v_off + h * dv:v_off + (h + 1) * dv],
                                             chunk, MODE_SINGLE) for h in range(heads)]

        def consume(chunks):
            for c in chunks:
                finish_chunk(proj, slice(c * chunk, (c + 1) * chunk), c // nch, staged.pop(c))

        for c0 in range(0, n_chunks, CHUNKS_PER_TRIP):
            chunks = list(range(c0, c0 + CHUNKS_PER_TRIP))
            items += [functools.partial(issue, chunks), functools.partial(consume, chunks)]
        return items

    def attention_loop(slot, mode):
        exact_diag = mode == MODE_PAIRWISE
        proj, _, q_view, k_view, b_view = views(slot)

        def first_stage(ci, copy):
            r0 = pl.multiple_of(ci * chunk, chunk)
            rs = pl.ds(r0, chunk)
            bc, kc = bc_ref.at[copy], kc_ref.at[copy]
            bc[...] = b_view[0][rs, b_view[1]:b_view[1] + kdim]
            if exact_diag:
                kc[...] = k_view[0][rs, k_view[1]:k_view[1] + kdim]
            return [_head_attention(q_view, k_view, b_view, bc, kc, r0, h, dk,
                                    proj[rs, v_off + h * dv:v_off + (h + 1) * dv],
                                    chunk, mode) for h in range(heads)]

        per_trip = 1 if exact_diag else CHUNKS_PER_TRIP

        def trip(ti, carry):
            staged = [first_stage(ti * per_trip + j, j) for j in range(per_trip)]
            for j in range(per_trip):
                ci = ti * per_trip + j
                finish_chunk(proj, pl.ds(pl.multiple_of(ci * chunk, chunk), chunk), ci // nch, staged[j])
            return carry
        lax.fori_loop(0, n_chunks // per_trip, trip, 0)

    prev_mode = flag_ref[prv]
    overlap = has_cur & has_prev & (prev_mode == MODE_SINGLE)

    def project_and_attend(slot):
        proj_items, record_bound = projection_items(slot)
        attn_items = attention_items(1 - slot)
        done = 0
        for i, attend in enumerate(attn_items):
            upto = -(-len(proj_items) * (i + 1) // len(attn_items))
            for item in proj_items[done:upto]:
                item()
            done = upto
            attend()
        record_bound()

    for slot in range(2):
        pl.when(overlap & (cur == slot))(functools.partial(project_and_attend, slot))

    @pl.when(jnp.logical_not(overlap))
    def _one_at_a_time():
        for mode in (MODE_SINGLE, MODE_BLOCKS, MODE_PAIRWISE):
            pl.when(has_prev & (prev_mode == mode))(functools.partial(attention_loop, prv, mode))

        @pl.when(has_cur)
        def _project():
            proj_items, record_bound = projection_items(cur)
            for item in proj_items:
                item()
            record_bound()

    @pl.when(has_prev)
    def _output():
        xp = xp_ref[...].reshape(rows, D_MODEL)
        y = _dot(o_ref[...], w_out_ref[...])
        out = _layer_norm(ALPHA * xp + y, lng_ref[...], lnb_ref[...])
        y_ref[...] = out.reshape(y_ref.shape)

    @pl.when(s == ns - 1)
    def _write_state():
        so_ref[...] = st_ref[...]


def _weight_operand(w):
    if isinstance(w, tuple):
        arr, layer = w
        nd = arr.ndim - 1
        return arr, pl.BlockSpec((None,) + arr.shape[1:], lambda b, s: (layer,) + (0,) * nd,
                                 pipeline_mode=pl.Buffered(1))
    nd = w.ndim
    return w, pl.BlockSpec(w.shape, lambda b, s: (0,) * nd, pipeline_mode=pl.Buffered(1))


def _mixer_call(kind, layer, x, s0, weights, *, bb, rows_t, chunk):
    B, T, D = x.shape
    heads, dk, dv = (HGRN_HEADS, HGRN_DK, HGRN_DV) if kind == "hgrn" else (GLA_HEADS, GLA_DK, GLA_DV)
    nch = rows_t // chunk
    rows = bb * rows_t
    nt = T // rows_t
    grid = (B // bb, nt + 1)
    has_s0 = s0 is not None

    def this_block(b, s):
        return (b, jnp.minimum(s, nt - 1), 0)

    def prev_block(b, s):
        return (b, jnp.maximum(s - 1, 0), 0)

    in_specs = [pl.BlockSpec((bb, rows_t, D), this_block),
                pl.BlockSpec((bb, rows_t, D), prev_block)]
    args = [x, x]
    if has_s0:
        in_specs.append(pl.BlockSpec((bb, heads, dk, dv), lambda b, s: (b, 0, 0, 0)))
        args.append(s0)
    for w in weights:
        arr, spec = _weight_operand(w)
        in_specs.append(spec)
        args.append(arr)

    ncols = 3 * HGRN_FDIM + D_MODEL if kind == "hgrn" else 2 * GLA_KEY_DIM + 2 * GLA_VAL_DIM
    scratch = [
        pltpu.VMEM((2, rows, ncols), F32),
        pltpu.VMEM((2, rows, heads * dk), F32),
        pltpu.VMEM((CHUNKS_PER_TRIP, chunk, heads * dk), F32),
        pltpu.VMEM((CHUNKS_PER_TRIP, chunk, heads * dk), F32),
        pltpu.VMEM((rows, heads * dv), BF16),
        pltpu.VMEM((bb, heads, dk, dv), F32),
        pltpu.SMEM((2,), jnp.int32),
    ]
    kern = functools.partial(_mixer_kernel, kind=kind, layer=layer, bb=bb, nch=nch, chunk=chunk,
                             has_s0=has_s0)
    return pl.pallas_call(
        kern,
        grid=grid,
        in_specs=in_specs,
        out_specs=[pl.BlockSpec((bb, rows_t, D), prev_block),
                   pl.BlockSpec((bb, heads, dk, dv), lambda b, s: (b, 0, 0, 0))],
        out_shape=[jax.ShapeDtypeStruct((B, T, D), F32),
                   jax.ShapeDtypeStruct((B, heads, dk, dv), F32)],
        scratch_shapes=scratch,
        compiler_params=pltpu.CompilerParams(
            dimension_semantics=("arbitrary", "arbitrary"),
            vmem_limit_bytes=VMEM_LIMIT_BYTES),
        name=f"{kind}_mixer_{'s' if has_s0 else 'p'}",
    )(*args)


def _ffn_kernel(*refs, bb, rows_t, has_s0):
    it = iter(refs)
    x_ref = next(it)
    c0_ref = next(it) if has_s0 else None
    w_up_ref = next(it)
    cw_ref = next(it)
    cb_ref = next(it)
    w_dn_ref = next(it)
    lng_ref = next(it)
    lnb_ref = next(it)
    y_ref = next(it)
    co_ref = next(it)
    a_ref = next(it)
    h_ref = next(it)
    acc_ref = next(it)

    rows = bb * rows_t
    t = pl.program_id(1)
    nt = pl.num_programs(1)
    pad = SUBLANES

    @pl.when(t == 0)
    def _init_carry():
        if has_s0:
            a_ref[:, pad - 2:pad, :] = c0_ref[...]
        else:
            a_ref[:, pad - 2:pad, :] = jnp.zeros((bb, 2, D_FF), F32)

    x = x_ref[...].reshape(rows, D_MODEL)
    xb = x.astype(BF16)

    def up_proj(n0):
        cs = slice(n0, n0 + FFN_COLS)
        a = _dot(xb, w_up_ref[:, cs])
        a_ref[:, pad:pad + rows_t, cs] = a.reshape(bb, rows_t, FFN_COLS)
        return _dot(xb, w_up_ref[:, D_FF + n0:D_FF + n0 + FFN_COLS])

    def activate(n0, g):
        cs = slice(n0, n0 + FFN_COLS)
        conv = (cb_ref[:, cs]
                + a_ref[:, pad - 2:pad - 2 + rows_t, cs] * cw_ref[0:1, cs]
                + a_ref[:, pad - 1:pad - 1 + rows_t, cs] * cw_ref[1:2, cs]
                + a_ref[:, pad:pad + rows_t, cs] * cw_ref[2:3, cs])
        h_ref[:, cs] = (_gelu_tanh(conv).reshape(rows, FFN_COLS) * g).astype(BF16)

    starts = list(range(0, D_FF, FFN_COLS))
    g_next = up_proj(starts[0])
    for idx, n0 in enumerate(starts):
        g = g_next
        if idx + 1 < len(starts):
            g_next = up_proj(starts[idx + 1])
        activate(n0, g)

    for n0 in range(0, D_MODEL, DOWN_COLS):
        acc_ref[:, n0:n0 + DOWN_COLS] = _dot(h_ref[...], w_dn_ref[:, n0:n0 + DOWN_COLS])

    out = _layer_norm(ALPHA * x + acc_ref[...], lng_ref[...], lnb_ref[...])
    y_ref[...] = out.reshape(y_ref.shape)

    last2 = a_ref[:, pad + rows_t - 2:pad + rows_t, :]
    a_ref[:, pad - 2:pad, :] = last2

    @pl.when(t == nt - 1)
    def _write_carry():
        co_ref[...] = last2


def _ffn_call(x, c0, weights, *, bb, rows_t):
    B, T, D = x.shape
    grid = (B // bb, T // rows_t)
    has_s0 = c0 is not None
    in_specs = [pl.BlockSpec((bb, rows_t, D), lambda b, t: (b, t, 0))]
    args = [x]
    if has_s0:
        in_specs.append(pl.BlockSpec((bb, CONV_WIDTH - 1, D_FF), lambda b, t: (b, 0, 0)))
        args.append(c0)
    for w in weights:
        arr, spec = _weight_operand(w)
        in_specs.append(spec)
        args.append(arr)
    kern = functools.partial(_ffn_kernel, bb=bb, rows_t=rows_t, has_s0=has_s0)
    return pl.pallas_call(
        kern,
        grid=grid,
        in_specs=in_specs,
        out_specs=[pl.BlockSpec((bb, rows_t, D), lambda b, t: (b, t, 0)),
                   pl.BlockSpec((bb, CONV_WIDTH - 1, D_FF), lambda b, t: (b, 0, 0))],
        out_shape=[jax.ShapeDtypeStruct((B, T, D), F32),
                   jax.ShapeDtypeStruct((B, CONV_WIDTH - 1, D_FF), F32)],
        scratch_shapes=[pltpu.VMEM((bb, SUBLANES + rows_t, D_FF), F32),
                        pltpu.VMEM((bb * rows_t, D_FF), BF16),
                        pltpu.VMEM((bb * rows_t, D), F32)],
        compiler_params=pltpu.CompilerParams(
            dimension_semantics=("arbitrary", "arbitrary"),
            vmem_limit_bytes=VMEM_LIMIT_BYTES),
        name=f"conv_ffn_{'s' if has_s0 else 'p'}",
    )(*args)


def _trunk(x, st_hgrn, st_gla, st_conv, layer_weights, *, mixer_bb, ffn_bb, rows_t, chunk):
    new_h, new_g, new_c = [], [], []
    for i in range(DEPTH):
        kind, mix_w, ffn_w = layer_weights[i]
        j = i // 2
        if kind == "hgrn":
            s0 = None if st_hgrn is None else st_hgrn[j]
        else:
            s0 = None if st_gla is None else st_gla[j]
        x, s = _mixer_call(kind, i, x, s0, mix_w, bb=mixer_bb, rows_t=rows_t, chunk=chunk)
        (new_h if kind == "hgrn" else new_g).append(s)
        c0 = None if st_conv is None else st_conv[i]
        x, c = _ffn_call(x, c0, ffn_w, bb=ffn_bb, rows_t=rows_t)
        new_c.append(c)
    return x, jnp.stack(new_h), jnp.stack(new_g), jnp.stack(new_c)


def kernel(x_prompt, x_sample, state_hgrn, state_gla, state_ffn_conv, lb_param, hgrn_w_in, hgrn_w_out, hgrn_norm_g, gla_w_in, gla_w_gk2, gla_b_gk2, gla_w_out, gla_norm_g, ln_mix_g, ln_mix_b, ffn_w_up, ffn_conv_w, ffn_conv_b, ffn_w_down, ln_ffn_g, ln_ffn_b):
    gla_main = 2 * GLA_KEY_DIM + 2 * GLA_VAL_DIM
    hgrn_w_in_b, hgrn_w_out_b = hgrn_w_in.astype(BF16), hgrn_w_out.astype(BF16)
    gla_w_in_b, gla_w_out_b = gla_w_in.astype(BF16), gla_w_out.astype(BF16)
    ffn_w_up_b, ffn_w_down_b = ffn_w_up.astype(BF16), ffn_w_down.astype(BF16)
    layer_weights = []
    for i in range(DEPTH):
        j = i // 2
        if i % 2 == 0:
            mix_w = (lb_param.astype(F32),
                     (hgrn_w_in_b, j),
                     (hgrn_w_out_b, j),
                     hgrn_norm_g[j].reshape(1, HGRN_DV),
                     ln_mix_g[i].reshape(1, D_MODEL),
                     ln_mix_b[i].reshape(1, D_MODEL))
            kind = "hgrn"
        else:
            w_r = jnp.pad(gla_w_in[j][:, gla_main:], ((0, 0), (0, LANES - GLA_GATE_RANK)))
            w_gk2 = jnp.pad(gla_w_gk2[j], ((0, LANES - GLA_GATE_RANK), (0, 0)))
            mix_w = ((gla_w_in_b, j),
                     w_r.astype(BF16),
                     w_gk2.astype(BF16),
                     gla_b_gk2[j].reshape(1, GLA_KEY_DIM),
                     (gla_w_out_b, j),
                     gla_norm_g[j].reshape(1, GLA_DV),
                     ln_mix_g[i].reshape(1, D_MODEL),
                     ln_mix_b[i].reshape(1, D_MODEL))
            kind = "gla"
        ffn_w = ((ffn_w_up_b, i),
                 ffn_conv_w[i],
                 ffn_conv_b[i].reshape(1, D_FF),
                 (ffn_w_down_b, i),
                 ln_ffn_g[i].reshape(1, D_MODEL),
                 ln_ffn_b[i].reshape(1, D_MODEL))
        layer_weights.append((kind, mix_w, ffn_w))

    y_p, h_p, g_p, c_p = _trunk(x_prompt, None, None, None, layer_weights,
                                mixer_bb=1, ffn_bb=1, rows_t=PROMPT_ROWS, chunk=CHUNK)
    dec_b, dec_t = x_sample.shape[0], x_sample.shape[1]
    y_s, h_s, g_s, c_s = _trunk(x_sample, state_hgrn, state_gla, state_ffn_conv, layer_weights,
                                mixer_bb=8, ffn_bb=dec_b, rows_t=dec_t, chunk=dec_t)
    return (y_p, y_s, h_p, h_s, g_p, g_s, c_p, c_s)
```

```python
import functools
import math

import jax
import jax.numpy as jnp
from jax import lax
from jax.experimental import pallas as pl
from jax.experimental.pallas import tpu as pltpu

F32 = jnp.float32
BF16 = jnp.bfloat16

D_MODEL = 1024
DEPTH = 2
CHUNK = 64
HGRN_HEADS = 8
HGRN_DK = 128
HGRN_DV = 128
HGRN_FDIM = HGRN_HEADS * HGRN_DK
GLA_HEADS = 4
GLA_KEY_DIM = 512
GLA_VAL_DIM = 1024
GLA_DK = 128
GLA_DV = 256
GLA_GATE_RANK = 16
GLA_GATE_NORMALIZER = 16.0
D_FF = 2816
CONV_WIDTH = 3
ALPHA = (2.0 * DEPTH) ** 0.25
LN_EPS = 1e-5
RMS_EPS = 1e-6
LOG2E = 1.4426950408889634

SUBLANES = 8
LANES = 128
VMEM_LIMIT_BYTES = 56 * 1024 * 1024
PROMPT_ROWS = 512
FFN_COLS = 256
DOWN_COLS = 512
PROJ_COLS = 512
MAX_FACTOR_LOG2 = 100.0
FAST_BLOCK_ROWS = 2 * SUBLANES
MODE_SINGLE, MODE_BLOCKS, MODE_PAIRWISE = 2, 1, 0
CHUNKS_PER_TRIP = 2
OUTPUT_OVERLAP_ITEMS = 2


def _dot(a, b):
    return jnp.dot(a, b, preferred_element_type=F32)


def _dot_nt(a, b):
    return lax.dot_general(a, b, (((1,), (1,)), ((), ())), preferred_element_type=F32)


def _dot_tn(a, b):
    return lax.dot_general(a, b, (((0,), (0,)), ((), ())), preferred_element_type=F32)


def _sigmoid(x):
    return 1.0 / (1.0 + jnp.exp(-x))


def _silu(x):
    return x * _sigmoid(x)


def _gelu_tanh(x):
    c = math.sqrt(2.0 / math.pi)
    k = -2.0 * c * LOG2E
    z = x * ((k * 0.044715) * (x * x) + k)
    return x / (1.0 + jnp.exp2(z))


def _log2_sigmoid(x):
    return jnp.minimum(x, 0.0) * LOG2E - jnp.log2(1.0 + jnp.exp2(jnp.abs(x) * (-LOG2E)))


def _layer_norm(x, g, b):
    mu = jnp.mean(x, axis=-1, keepdims=True)
    xc = x - mu
    var = jnp.mean(xc * xc, axis=-1, keepdims=True)
    return xc * lax.rsqrt(var + LN_EPS) * g + b


def _cumsum_rows(g):
    c = g.shape[0]
    r = lax.broadcasted_iota(jnp.int32, (c, c), 0)
    s = lax.broadcasted_iota(jnp.int32, (c, c), 1)
    tri = (s <= r).astype(BF16)
    hi = g.astype(BF16)
    lo = (g - hi.astype(F32)).astype(BF16)
    return _dot(tri, hi) + _dot(tri, lo)


def _mode_limits(chunk):
    half_windows = max(chunk // 2 // SUBLANES, 1)
    block_windows = min(FAST_BLOCK_ROWS, chunk) // SUBLANES
    return MAX_FACTOR_LOG2 / half_windows, MAX_FACTOR_LOG2 / block_windows


def _head_attention(q_view, k_view, b_view, bc_ref, kc_ref, r0, h, dk, v, chunk, mode):
    exact_diag = mode == MODE_PAIRWISE
    if mode == MODE_PAIRWISE:
        R = SUBLANES
    elif mode == MODE_BLOCKS:
        R = min(FAST_BLOCK_ROWS, chunk)
    else:
        R = chunk
    ref_in_block = R // 2 - 1 if mode == MODE_SINGLE else R - 1
    nb = chunk // R
    static_rows = isinstance(r0, int)

    def cols(view):
        return slice(view[1] + h * dk, view[1] + (h + 1) * dk)

    def blk(view, i):
        if static_rows:
            return view[0][r0 + R * i:r0 + R * (i + 1), cols(view)]
        return view[0][pl.ds(pl.multiple_of(r0 + R * i, R), R), cols(view)]

    def row(view, copy_ref, t):
        if static_rows:
            one = view[0][r0 + t:r0 + t + 1, cols(view)]
        else:
            one = copy_ref[t:t + 1, h * dk:(h + 1) * dk]
        return jnp.broadcast_to(one, (R, dk))

    def full(view):
        if static_rows:
            return view[0][r0:r0 + chunk, cols(view)]
        return view[0][pl.ds(r0, chunk), cols(view)]

    def b_end(j):
        return row(b_view, bc_ref, R * j + ref_in_block)

    kpp = jnp.concatenate(
        [blk(k_view, j) * jnp.exp2(b_end(j) - blk(b_view, j)) for j in range(nb)],
        axis=0).astype(BF16)

    col = lax.broadcasted_iota(jnp.int32, (R, chunk), 1)
    rw = lax.broadcasted_iota(jnp.int32, (R, chunk), 0)
    colblk = col // R

    lq, pair = [], {}
    for i in range(nb):
        qi, bi = blk(q_view, i), blk(b_view, i)
        for j in range(i if exact_diag else i + 1):
            pair[(i, j)] = len(lq)
            lq.append(qi * jnp.exp2(bi - b_end(j)))
    if lq:
        cross = _dot_nt(jnp.concatenate(lq, axis=0).astype(BF16), kpp)

    def cross_blk(i, j):
        p = pair[(i, j)]
        return cross[R * p:R * (p + 1), :]

    b_all = full(b_view)
    if static_rows:
        b_last = b_view[0][r0 + chunk - 1:r0 + chunk, cols(b_view)]
    else:
        b_last = bc_ref[chunk - 1:chunk, h * dk:(h + 1) * dk]
    vb = v.astype(BF16)
    qe = (full(q_view) * jnp.exp2(b_all)).astype(BF16)
    kd = (full(k_view) * jnp.exp2(b_last - b_all)).astype(BF16)
    kv = _dot_tn(kd, vb)
    dv = v.shape[1]
    decay = jnp.broadcast_to(jnp.exp2(b_last), (dk, dk)).T
    if dv > dk:
        decay = jnp.concatenate([decay] * (dv // dk), axis=1)

    def finish(st):
        a_rows = []
        for i in range(nb):
            if exact_diag:
                acc = jnp.zeros((R, chunk), F32)
                for j in range(i):
                    acc = jnp.where(colblk == j, cross_blk(i, j), acc)
                qi, bi = blk(q_view, i), blk(b_view, i)
                for s in range(R):
                    t = R * i + s
                    e = jnp.exp2(jnp.minimum(bi - row(b_view, bc_ref, t), 0.0))
                    cs = jnp.sum(qi * e * row(k_view, kc_ref, t), axis=-1, keepdims=True)
                    acc = jnp.where(col == t, cs, acc)
            else:
                acc = cross_blk(i, 0)
                for j in range(1, i + 1):
                    acc = jnp.where(colblk == j, cross_blk(i, j), acc)
            a_rows.append(jnp.where(col <= R * i + rw, acc, 0.0))
        a = jnp.concatenate(a_rows, axis=0).astype(BF16)
        o = _dot(jnp.concatenate([qe, a], axis=1),
                 jnp.concatenate([st.astype(BF16), vb], axis=0))
        st_new = decay * st + kv
        return o, st_new

    return finish


def _rms_gate(o, norm_g, gate_act):
    ms = jnp.mean(o * o, axis=-1, keepdims=True)
    return o * lax.rsqrt(ms + RMS_EPS) * norm_g * gate_act


def _mixer_kernel(*refs, kind, layer, bb, nch, chunk, has_s0):
    it = iter(refs)
    x_ref = next(it)
    xp_ref = next(it)
    s0_ref = next(it) if has_s0 else None
    if kind == "hgrn":
        lb_ref = next(it)
        w_in_ref = next(it)
    else:
        w_in_ref = next(it)
        w_r_ref = next(it)
        w_gk2_ref = next(it)
        b_gk2_ref = next(it)
    w_out_ref = next(it)
    ng_ref = next(it)
    lng_ref = next(it)
    lnb_ref = next(it)
    y_ref = next(it)
    so_ref = next(it)
    proj2_ref = next(it)
    aux2_ref = next(it)
    bc_ref = next(it)
    kc_ref = next(it)
    o_ref = next(it)
    st_ref = next(it)
    flag_ref = next(it)

    heads, dk, dv = (HGRN_HEADS, HGRN_DK, HGRN_DV) if kind == "hgrn" else (GLA_HEADS, GLA_DK, GLA_DV)
    kdim = heads * dk
    rows = bb * nch * chunk
    n_chunks = bb * nch
    nb = chunk // SUBLANES
    ncols = proj2_ref.shape[2]
    s = pl.program_id(1)
    ns = pl.num_programs(1)
    cur = s % 2
    prv = 1 - cur
    has_cur = s < ns - 1
    has_prev = s > 0

    if kind == "hgrn":
        q_off, f_off, v_off, g_off = 0, kdim, 2 * kdim, 3 * kdim
    else:
        q_off, k_off, v_off, g_off = 0, kdim, 2 * kdim, 2 * kdim + heads * dv

    def views(slot):
        proj, aux = proj2_ref.at[slot], aux2_ref.at[slot]
        if kind == "hgrn":
            return proj, aux, (proj, q_off), (aux, 0), (proj, f_off)
        return proj, aux, (proj, q_off), (proj, k_off), (aux, 0)

    @pl.when(s == 0)
    def _init():
        flag_ref[0] = 0
        flag_ref[1] = 0
        if has_s0:
            st_ref[...] = s0_ref[...]
        else:
            st_ref[...] = jnp.zeros(st_ref.shape, F32)

    ng = ng_ref[...]

    def projection_items(slot):
        proj, aux, _, _, _ = views(slot)
        xb = x_ref[...].reshape(rows, D_MODEL).astype(BF16)
        decay_max = []

        def cumsum_in_place(ref, cs):
            m = jnp.zeros((SUBLANES, cs.stop - cs.start), F32)
            for c in range(n_chunks):
                rs = slice(c * chunk, (c + 1) * chunk)
                b = _cumsum_rows(ref[rs, cs])
                ref[rs, cs] = b
                m = jnp.maximum(m, -b[0:SUBLANES])
                for j in range(1, nb):
                    m = jnp.maximum(m, b[SUBLANES * (j - 1):SUBLANES * j] - b[SUBLANES * j:SUBLANES * (j + 1)])
            decay_max.append(jnp.max(m))

        if kind == "hgrn":
            p = lb_ref[...]
            pe = jnp.exp(p - jnp.max(p, axis=0, keepdims=True))
            lb = jnp.sum(pe[:layer + 1], axis=0, keepdims=True) / jnp.sum(pe, axis=0, keepdims=True)

        def col_block(n0):
            cs = slice(n0, n0 + PROJ_COLS)
            blk = _dot(xb, w_in_ref[:, cs])
            if n0 >= g_off:
                proj[:, cs] = _silu(blk)
            elif n0 >= v_off:
                proj[:, cs] = blk
            elif kind == "hgrn" and n0 >= f_off:
                fs = slice(n0 - f_off, n0 - f_off + PROJ_COLS)
                f = lb[:, fs] + (1.0 - lb[:, fs]) * _sigmoid(blk)
                aux[:, fs] = 1.0 - f
                proj[:, cs] = jnp.log2(f)
                cumsum_in_place(proj, cs)
            elif kind == "hgrn":
                proj[:, cs] = _silu(blk)
            elif n0 >= k_off:
                proj[:, cs] = blk
            else:
                proj[:, cs] = blk * (dk ** -0.5)

        def gla_gate():
            r = _dot(xb, w_r_ref[...]).astype(BF16)
            z = _dot(r, w_gk2_ref[...]) + b_gk2_ref[...]
            aux[...] = _log2_sigmoid(z) * (1.0 / GLA_GATE_NORMALIZER)
            cumsum_in_place(aux, slice(0, kdim))

        def record_bound():
            worst = functools.reduce(jnp.maximum, decay_max)
            single_max, blocks_max = _mode_limits(chunk)
            flag_ref[slot] = jnp.where(
                worst <= single_max, MODE_SINGLE,
                jnp.where(worst <= blocks_max, MODE_BLOCKS, MODE_PAIRWISE)).astype(jnp.int32)

        items = [functools.partial(col_block, n0) for n0 in range(0, ncols, PROJ_COLS)]
        if kind == "gla":
            items.append(gla_gate)
        return items, record_bound

    def finish_chunk(proj, c_rows, b_idx, finishers):
        for h, finish in enumerate(finishers):
            o, st_new = finish(st_ref[b_idx, h])
            st_ref[b_idx, h] = st_new
            gate = proj[c_rows, g_off + h * dv:g_off + (h + 1) * dv]
            o_ref[c_rows, h * dv:(h + 1) * dv] = _rms_gate(o, ng, gate).astype(BF16)

    def attention_items(slot):
        proj, _, q_view, k_view, b_view = views(slot)
        items, staged = [], {}

        def issue(chunks):
            for c in chunks:
                r0 = c * chunk
                staged[c] = [_head_attention(q_view, k_view, b_view, None, None, r0, h, dk,
                                             proj[r0:r0 + chunk, v_off + h * dv:v_off + (h + 1) * dv],
                                             chunk, MODE_SINGLE) for h in range(heads)]

        def consume(chunks):
            for c in chunks:
                finish_chunk(proj, slice(c * chunk, (c + 1) * chunk), c // nch, staged.pop(c))

        for c0 in range(0, n_chunks, CHUNKS_PER_TRIP):
            chunks = list(range(c0, c0 + CHUNKS_PER_TRIP))
            items += [functools.partial(issue, chunks), functools.partial(consume, chunks)]
        return items

    def attention_loop(slot, mode):
        exact_diag = mode == MODE_PAIRWISE
        proj, _, q_view, k_view, b_view = views(slot)

        def first_stage(ci, copy):
            r0 = pl.multiple_of(ci * chunk, chunk)
            rs = pl.ds(r0, chunk)
            bc, kc = bc_ref.at[copy], kc_ref.at[copy]
            bc[...] = b_view[0][rs, b_view[1]:b_view[1] + kdim]
            if exact_diag:
                kc[...] = k_view[0][rs, k_view[1]:k_view[1] + kdim]
            return [_head_attention(q_view, k_view, b_view, bc, kc, r0, h, dk,
                                    proj[rs, v_off + h * dv:v_off + (h + 1) * dv],
                                    chunk, mode) for h in range(heads)]

        per_trip = 1 if exact_diag else CHUNKS_PER_TRIP

        def trip(ti, carry):
            staged = [first_stage(ti * per_trip + j, j) for j in range(per_trip)]
            for j in range(per_trip):
                ci = ti * per_trip + j
                finish_chunk(proj, pl.ds(pl.multiple_of(ci * chunk, chunk), chunk), ci // nch, staged[j])
            return carry
        lax.fori_loop(0, n_chunks // per_trip, trip, 0)

    prev_mode = flag_ref[prv]
    overlap = has_cur & has_prev & (prev_mode == MODE_SINGLE)

    def output_matmul():
        return _dot(o_ref[...], w_out_ref[...])

    def output_norm(y):
        xp = xp_ref[...].reshape(rows, D_MODEL)
        out = _layer_norm(ALPHA * xp + y, lng_ref[...], lnb_ref[...])
        y_ref[...] = out.reshape(y_ref.shape)

    def project_and_attend(slot):
        proj_items, record_bound = projection_items(slot)
        attn_items = attention_items(1 - slot)
        held_back = proj_items[len(proj_items) - OUTPUT_OVERLAP_ITEMS:]
        proj_items = proj_items[:len(proj_items) - OUTPUT_OVERLAP_ITEMS]
        done = 0
        for i, attend in enumerate(attn_items):
            upto = -(-len(proj_items) * (i + 1) // len(attn_items))
            for item in proj_items[done:upto]:
                item()
            done = upto
            attend()
        y = output_matmul()
        for item in held_back[:-1]:
            item()
        output_norm(y)
        held_back[-1]()
        record_bound()

    for slot in range(2):
        pl.when(overlap & (cur == slot))(functools.partial(project_and_attend, slot))

    @pl.when(jnp.logical_not(overlap))
    def _one_at_a_time():
        for mode in (MODE_SINGLE, MODE_BLOCKS, MODE_PAIRWISE):
            pl.when(has_prev & (prev_mode == mode))(functools.partial(attention_loop, prv, mode))

        @pl.when(has_cur)
        def _project():
            proj_items, record_bound = projection_items(cur)
            for item in proj_items:
                item()
            record_bound()

    @pl.when(has_prev & jnp.logical_not(overlap))
    def _output():
        output_norm(output_matmul())

    @pl.when(s == ns - 1)
    def _write_state():
        so_ref[...] = st_ref[...]


def _weight_operand(w):
    if isinstance(w, tuple):
        arr, layer = w
        nd = arr.ndim - 1
        return arr, pl.BlockSpec((None,) + arr.shape[1:], lambda b, s: (layer,) + (0,) * nd,
                                 pipeline_mode=pl.Buffered(1))
    nd = w.ndim
    return w, pl.BlockSpec(w.shape, lambda b, s: (0,) * nd, pipeline_mode=pl.Buffered(1))


def _mixer_call(kind, layer, x, s0, weights, *, bb, rows_t, chunk):
    B, T, D = x.shape
    heads, dk, dv = (HGRN_HEADS, HGRN_DK, HGRN_DV) if kind == "hgrn" else (GLA_HEADS, GLA_DK, GLA_DV)
    nch = rows_t // chunk
    rows = bb * rows_t
    nt = T // rows_t
    grid = (B // bb, nt + 1)
    has_s0 = s0 is not None

    def this_block(b, s):
        return (b, jnp.minimum(s, nt - 1), 0)

    def prev_block(b, s):
        return (b, jnp.maximum(s - 1, 0), 0)

    in_specs = [pl.BlockSpec((bb, rows_t, D), this_block),
                pl.BlockSpec((bb, rows_t, D), prev_block)]
    args = [x, x]
    if has_s0:
        in_specs.append(pl.BlockSpec((bb, heads, dk, dv), lambda b, s: (b, 0, 0, 0)))
        args.append(s0)
    for w in weights:
        arr, spec = _weight_operand(w)
        in_specs.append(spec)
        args.append(arr)

    ncols = 3 * HGRN_FDIM + D_MODEL if kind == "hgrn" else 2 * GLA_KEY_DIM + 2 * GLA_VAL_DIM
    scratch = [
        pltpu.VMEM((2, rows, ncols), F32),
        pltpu.VMEM((2, rows, heads * dk), F32),
        pltpu.VMEM((CHUNKS_PER_TRIP, chunk, heads * dk), F32),
        pltpu.VMEM((CHUNKS_PER_TRIP, chunk, heads * dk), F32),
        pltpu.VMEM((rows, heads * dv), BF16),
        pltpu.VMEM((bb, heads, dk, dv), F32),
        pltpu.SMEM((2,), jnp.int32),
    ]
    kern = functools.partial(_mixer_kernel, kind=kind, layer=layer, bb=bb, nch=nch, chunk=chunk,
                             has_s0=has_s0)
    return pl.pallas_call(
        kern,
        grid=grid,
        in_specs=in_specs,
        out_specs=[pl.BlockSpec((bb, rows_t, D), prev_block),
                   pl.BlockSpec((bb, heads, dk, dv), lambda b, s: (b, 0, 0, 0))],
        out_shape=[jax.ShapeDtypeStruct((B, T, D), F32),
                   jax.ShapeDtypeStruct((B, heads, dk, dv), F32)],
        scratch_shapes=scratch,
        compiler_params=pltpu.CompilerParams(
            dimension_semantics=("arbitrary", "arbitrary"),
            vmem_limit_bytes=VMEM_LIMIT_BYTES),
        name=f"{kind}_mixer_{'s' if has_s0 else 'p'}",
    )(*args)


def _ffn_kernel(*refs, bb, rows_t, has_s0):
    it = iter(refs)
    x_ref = next(it)
    c0_ref = next(it) if has_s0 else None
    w_up_ref = next(it)
    cw_ref = next(it)
    cb_ref = next(it)
    w_dn_ref = next(it)
    lng_ref = next(it)
    lnb_ref = next(it)
    y_ref = next(it)
    co_ref = next(it)
    a_ref = next(it)
    h_ref = next(it)
    acc_ref = next(it)

    rows = bb * rows_t
    t = pl.program_id(1)
    nt = pl.num_programs(1)
    pad = SUBLANES

    @pl.when(t == 0)
    def _init_carry():
        if has_s0:
            a_ref[:, pad - 2:pad, :] = c0_ref[...]
        else:
            a_ref[:, pad - 2:pad, :] = jnp.zeros((bb, 2, D_FF), F32)

    x = x_ref[...].reshape(rows, D_MODEL)
    xb = x.astype(BF16)

    def up_proj(n0):
        cs = slice(n0, n0 + FFN_COLS)
        a = _dot(xb, w_up_ref[:, cs])
        a_ref[:, pad:pad + rows_t, cs] = a.reshape(bb, rows_t, FFN_COLS)
        return _dot(xb, w_up_ref[:, D_FF + n0:D_FF + n0 + FFN_COLS])

    def activate(n0, g):
        cs = slice(n0, n0 + FFN_COLS)
        conv = (cb_ref[:, cs]
                + a_ref[:, pad - 2:pad - 2 + rows_t, cs] * cw_ref[0:1, cs]
                + a_ref[:, pad - 1:pad - 1 + rows_t, cs] * cw_ref[1:2, cs]
                + a_ref[:, pad:pad + rows_t, cs] * cw_ref[2:3, cs])
        h_ref[:, cs] = (_gelu_tanh(conv).reshape(rows, FFN_COLS) * g).astype(BF16)

    starts = list(range(0, D_FF, FFN_COLS))
    g_next = up_proj(starts[0])
    for idx, n0 in enumerate(starts):
        g = g_next
        if idx + 1 < len(starts):
            g_next = up_proj(starts[idx + 1])
        activate(n0, g)

    for n0 in range(0, D_MODEL, DOWN_COLS):
        acc_ref[:, n0:n0 + DOWN_COLS] = _dot(h_ref[...], w_dn_ref[:, n0:n0 + DOWN_COLS])

    out = _layer_norm(ALPHA * x + acc_ref[...], lng_ref[...], lnb_ref[...])
    y_ref[...] = out.reshape(y_ref.shape)

    last2 = a_ref[:, pad + rows_t - 2:pad + rows_t, :]
    a_ref[:, pad - 2:pad, :] = last2

    @pl.when(t == nt - 1)
    def _write_carry():
        co_ref[...] = last2


def _ffn_call(x, c0, weights, *, bb, rows_t):
    B, T, D = x.shape
    grid = (B // bb, T // rows_t)
    has_s0 = c0 is not None
    in_specs = [pl.BlockSpec((bb, rows_t, D), lambda b, t: (b, t, 0))]
    args = [x]
    if has_s0:
        in_specs.append(pl.BlockSpec((bb, CONV_WIDTH - 1, D_FF), lambda b, t: (b, 0, 0)))
        args.append(c0)
    for w in weights:
        arr, spec = _weight_operand(w)
        in_specs.append(spec)
        args.append(arr)
    kern = functools.partial(_ffn_kernel, bb=bb, rows_t=rows_t, has_s0=has_s0)
    return pl.pallas_call(
        kern,
        grid=grid,
        in_specs=in_specs,
        out_specs=[pl.BlockSpec((bb, rows_t, D), lambda b, t: (b, t, 0)),
                   pl.BlockSpec((bb, CONV_WIDTH - 1, D_FF), lambda b, t: (b, 0, 0))],
        out_shape=[jax.ShapeDtypeStruct((B, T, D), F32),
                   jax.ShapeDtypeStruct((B, CONV_WIDTH - 1, D_FF), F32)],
        scratch_shapes=[pltpu.VMEM((bb, SUBLANES + rows_t, D_FF), F32),
                        pltpu.VMEM((bb * rows_t, D_FF), BF16),
                        pltpu.VMEM((bb * rows_t, D), F32)],
        compiler_params=pltpu.CompilerParams(
            dimension_semantics=("arbitrary", "arbitrary"),
            vmem_limit_bytes=VMEM_LIMIT_BYTES),
        name=f"conv_ffn_{'s' if has_s0 else 'p'}",
    )(*args)


def _trunk(x, st_hgrn, st_gla, st_conv, layer_weights, *, mixer_bb, ffn_bb, rows_t, chunk):
    new_h, new_g, new_c = [], [], []
    for i in range(DEPTH):
        kind, mix_w, ffn_w = layer_weights[i]
        j = i // 2
        if kind == "hgrn":
            s0 = None if st_hgrn is None else st_hgrn[j]
        else:
            s0 = None if st_gla is None else st_gla[j]
        x, s = _mixer_call(kind, i, x, s0, mix_w, bb=mixer_bb, rows_t=rows_t, chunk=chunk)
        (new_h if kind == "hgrn" else new_g).append(s)
        c0 = None if st_conv is None else st_conv[i]
        x, c = _ffn_call(x, c0, ffn_w, bb=ffn_bb, rows_t=rows_t)
        new_c.append(c)
    return x, jnp.stack(new_h), jnp.stack(new_g), jnp.stack(new_c)


def kernel(x_prompt, x_sample, state_hgrn, state_gla, state_ffn_conv, lb_param, hgrn_w_in, hgrn_w_out, hgrn_norm_g, gla_w_in, gla_w_gk2, gla_b_gk2, gla_w_out, gla_norm_g, ln_mix_g, ln_mix_b, ffn_w_up, ffn_conv_w, ffn_conv_b, ffn_w_down, ln_ffn_g, ln_ffn_b):
    gla_main = 2 * GLA_KEY_DIM + 2 * GLA_VAL_DIM
    hgrn_w_in_b, hgrn_w_out_b = hgrn_w_in.astype(BF16), hgrn_w_out.astype(BF16)
    gla_w_in_b, gla_w_out_b = gla_w_in.astype(BF16), gla_w_out.astype(BF16)
    ffn_w_up_b, ffn_w_down_b = ffn_w_up.astype(BF16), ffn_w_down.astype(BF16)
    layer_weights = []
    for i in range(DEPTH):
        j = i // 2
        if i % 2 == 0:
            mix_w = (lb_param.astype(F32),
                     (hgrn_w_in_b, j),
                     (hgrn_w_out_b, j),
                     hgrn_norm_g[j].reshape(1, HGRN_DV),
                     ln_mix_g[i].reshape(1, D_MODEL),
                     ln_mix_b[i].reshape(1, D_MODEL))
            kind = "hgrn"
        else:
            w_r = jnp.pad(gla_w_in[j][:, gla_main:], ((0, 0), (0, LANES - GLA_GATE_RANK)))
            w_gk2 = jnp.pad(gla_w_gk2[j], ((0, LANES - GLA_GATE_RANK), (0, 0)))
            mix_w = ((gla_w_in_b, j),
                     w_r.astype(BF16),
                     w_gk2.astype(BF16),
                     gla_b_gk2[j].reshape(1, GLA_KEY_DIM),
                     (gla_w_out_b, j),
                     gla_norm_g[j].reshape(1, GLA_DV),
                     ln_mix_g[i].reshape(1, D_MODEL),
                     ln_mix_b[i].reshape(1, D_MODEL))
            kind = "gla"
        ffn_w = ((ffn_w_up_b, i),
                 ffn_conv_w[i],
                 ffn_conv_b[i].reshape(1, D_FF),
                 (ffn_w_down_b, i),
                 ln_ffn_g[i].reshape(1, D_MODEL),
                 ln_ffn_b[i].reshape(1, D_MODEL))
        layer_weights.append((kind, mix_w, ffn_w))

    y_p, h_p, g_p, c_p = _trunk(x_prompt, None, None, None, layer_weights,
                                mixer_bb=1, ffn_bb=1, rows_t=PROMPT_ROWS, chunk=CHUNK)
    dec_b, dec_t = x_sample.shape[0], x_sample.shape[1]
    y_s, h_s, g_s, c_s = _trunk(x_sample, state_hgrn, state_gla, state_ffn_conv, layer_weights,
                                mixer_bb=8, ffn_bb=dec_b, rows_t=dec_t, chunk=dec_t)
    return (y_p, y_s, h_p, h_s, g_p, g_s, c_p, c_s)
```

```python
import functools
import math

import jax
import jax.numpy as jnp
from jax import lax
from jax.experimental import pallas as pl
from jax.experimental.pallas import tpu as pltpu

F32 = jnp.float32
BF16 = jnp.bfloat16

D_MODEL = 1024
DEPTH = 2
CHUNK = 64
HGRN_HEADS = 8
HGRN_DK = 128
HGRN_DV = 128
HGRN_FDIM = HGRN_HEADS * HGRN_DK
GLA_HEADS = 4
GLA_KEY_DIM = 512
GLA_VAL_DIM = 1024
GLA_DK = 128
GLA_DV = 256
GLA_GATE_RANK = 16
GLA_GATE_NORMALIZER = 16.0
D_FF = 2816
CONV_WIDTH = 3
ALPHA = (2.0 * DEPTH) ** 0.25
LN_EPS = 1e-5
RMS_EPS = 1e-6
LOG2E = 1.4426950408889634

SUBLANES = 8
LANES = 128
VMEM_LIMIT_BYTES = 56 * 1024 * 1024
PROMPT_ROWS = 512
FFN_COLS = 256
DOWN_COLS = 512
PROJ_COLS = 512
MAX_FACTOR_LOG2 = 100.0
FAST_BLOCK_ROWS = 2 * SUBLANES
MODE_SINGLE, MODE_BLOCKS, MODE_PAIRWISE = 2, 1, 0
TRIP_ROWS = 128


def _chunks_per_trip(chunk, n_chunks):
    return min(n_chunks, max(TRIP_ROWS // chunk, 1))


OUTPUT_OVERLAP_ITEMS = 2


def _dot(a, b):
    return jnp.dot(a, b, preferred_element_type=F32)


def _dot_nt(a, b):
    return lax.dot_general(a, b, (((1,), (1,)), ((), ())), preferred_element_type=F32)


def _dot_tn(a, b):
    return lax.dot_general(a, b, (((0,), (0,)), ((), ())), preferred_element_type=F32)


def _sigmoid(x):
    return 1.0 / (1.0 + jnp.exp(-x))


def _silu(x):
    return x * _sigmoid(x)


def _gelu_tanh(x):
    c = math.sqrt(2.0 / math.pi)
    k = -2.0 * c * LOG2E
    z = x * ((k * 0.044715) * (x * x) + k)
    return x / (1.0 + jnp.exp2(z))


def _log2_sigmoid(x):
    return jnp.minimum(x, 0.0) * LOG2E - jnp.log2(1.0 + jnp.exp2(jnp.abs(x) * (-LOG2E)))


def _layer_norm(x, g, b):
    mu = jnp.mean(x, axis=-1, keepdims=True)
    xc = x - mu
    var = jnp.mean(xc * xc, axis=-1, keepdims=True)
    return xc * lax.rsqrt(var + LN_EPS) * g + b


def _cumsum_rows(g):
    c = g.shape[0]
    r = lax.broadcasted_iota(jnp.int32, (c, c), 0)
    s = lax.broadcasted_iota(jnp.int32, (c, c), 1)
    tri = (s <= r).astype(BF16)
    hi = g.astype(BF16)
    lo = (g - hi.astype(F32)).astype(BF16)
    return _dot(tri, hi) + _dot(tri, lo)


def _mode_limits(chunk):
    half_windows = max(chunk // 2 // SUBLANES, 1)
    block_windows = min(FAST_BLOCK_ROWS, chunk) // SUBLANES
    return MAX_FACTOR_LOG2 / half_windows, MAX_FACTOR_LOG2 / block_windows


def _head_attention(q_view, k_view, b_view, bc_ref, kc_ref, r0, h, dk, v, chunk, mode):
    exact_diag = mode == MODE_PAIRWISE
    if mode == MODE_PAIRWISE:
        R = SUBLANES
    elif mode == MODE_BLOCKS:
        R = min(FAST_BLOCK_ROWS, chunk)
    else:
        R = chunk
    ref_in_block = R // 2 - 1 if mode == MODE_SINGLE else R - 1
    nb = chunk // R
    static_rows = isinstance(r0, int)

    def cols(view):
        return slice(view[1] + h * dk, view[1] + (h + 1) * dk)

    def blk(view, i):
        if static_rows:
            return view[0][r0 + R * i:r0 + R * (i + 1), cols(view)]
        return view[0][pl.ds(pl.multiple_of(r0 + R * i, R), R), cols(view)]

    def row(view, copy_ref, t):
        if static_rows:
            one = view[0][r0 + t:r0 + t + 1, cols(view)]
        else:
            one = copy_ref[t:t + 1, h * dk:(h + 1) * dk]
        return jnp.broadcast_to(one, (R, dk))

    def full(view):
        if static_rows:
            return view[0][r0:r0 + chunk, cols(view)]
        return view[0][pl.ds(r0, chunk), cols(view)]

    def b_end(j):
        return row(b_view, bc_ref, R * j + ref_in_block)

    kpp = jnp.concatenate(
        [blk(k_view, j) * jnp.exp2(b_end(j) - blk(b_view, j)) for j in range(nb)],
        axis=0).astype(BF16)

    col = lax.broadcasted_iota(jnp.int32, (R, chunk), 1)
    rw = lax.broadcasted_iota(jnp.int32, (R, chunk), 0)
    colblk = col // R

    lq, pair = [], {}
    for i in range(nb):
        qi, bi = blk(q_view, i), blk(b_view, i)
        for j in range(i if exact_diag else i + 1):
            pair[(i, j)] = len(lq)
            lq.append(qi * jnp.exp2(bi - b_end(j)))
    if lq:
        cross = _dot_nt(jnp.concatenate(lq, axis=0).astype(BF16), kpp)

    def cross_blk(i, j):
        p = pair[(i, j)]
        return cross[R * p:R * (p + 1), :]

    b_all = full(b_view)
    if static_rows:
        b_last = b_view[0][r0 + chunk - 1:r0 + chunk, cols(b_view)]
    else:
        b_last = bc_ref[chunk - 1:chunk, h * dk:(h + 1) * dk]
    vb = v.astype(BF16)
    qe = (full(q_view) * jnp.exp2(b_all)).astype(BF16)
    kd = (full(k_view) * jnp.exp2(b_last - b_all)).astype(BF16)
    kv = _dot_tn(kd, vb)
    dv = v.shape[1]
    decay = jnp.broadcast_to(jnp.exp2(b_last), (dk, dk)).T
    if dv > dk:
        decay = jnp.concatenate([decay] * (dv // dk), axis=1)

    def finish(st):
        a_rows = []
        for i in range(nb):
            if exact_diag:
                acc = jnp.zeros((R, chunk), F32)
                for j in range(i):
                    acc = jnp.where(colblk == j, cross_blk(i, j), acc)
                qi, bi = blk(q_view, i), blk(b_view, i)
                for s in range(R):
                    t = R * i + s
                    e = jnp.exp2(jnp.minimum(bi - row(b_view, bc_ref, t), 0.0))
                    cs = jnp.sum(qi * e * row(k_view, kc_ref, t), axis=-1, keepdims=True)
                    acc = jnp.where(col == t, cs, acc)
            else:
                acc = cross_blk(i, 0)
                for j in range(1, i + 1):
                    acc = jnp.where(colblk == j, cross_blk(i, j), acc)
            a_rows.append(jnp.where(col <= R * i + rw, acc, 0.0))
        a = jnp.concatenate(a_rows, axis=0).astype(BF16)
        o = _dot(jnp.concatenate([qe, a], axis=1),
                 jnp.concatenate([st.astype(BF16), vb], axis=0))
        st_new = decay * st + kv
        return o, st_new

    return finish


def _rms_gate(o, norm_g, gate_act):
    ms = jnp.mean(o * o, axis=-1, keepdims=True)
    return o * lax.rsqrt(ms + RMS_EPS) * norm_g * gate_act


def _mixer_kernel(*refs, kind, layer, bb, nch, chunk, has_s0):
    it = iter(refs)
    x_ref = next(it)
    xp_ref = next(it)
    s0_ref = next(it) if has_s0 else None
    if kind == "hgrn":
        lb_ref = next(it)
        w_in_ref = next(it)
    else:
        w_in_ref = next(it)
        w_r_ref = next(it)
        w_gk2_ref = next(it)
        b_gk2_ref = next(it)
    w_out_ref = next(it)
    ng_ref = next(it)
    lng_ref = next(it)
    lnb_ref = next(it)
    y_ref = next(it)
    so_ref = next(it)
    proj2_ref = next(it)
    aux2_ref = next(it)
    bc_ref = next(it)
    kc_ref = next(it)
    o_ref = next(it)
    st_ref = next(it)
    flag_ref = next(it)

    heads, dk, dv = (HGRN_HEADS, HGRN_DK, HGRN_DV) if kind == "hgrn" else (GLA_HEADS, GLA_DK, GLA_DV)
    kdim = heads * dk
    rows = bb * nch * chunk
    n_chunks = bb * nch
    per_item = _chunks_per_trip(chunk, n_chunks)
    nb = chunk // SUBLANES
    ncols = proj2_ref.shape[2]
    s = pl.program_id(1)
    ns = pl.num_programs(1)
    cur = s % 2
    prv = 1 - cur
    has_cur = s < ns - 1
    has_prev = s > 0

    if kind == "hgrn":
        q_off, f_off, v_off, g_off = 0, kdim, 2 * kdim, 3 * kdim
    else:
        q_off, k_off, v_off, g_off = 0, kdim, 2 * kdim, 2 * kdim + heads * dv

    def views(slot):
        proj, aux = proj2_ref.at[slot], aux2_ref.at[slot]
        if kind == "hgrn":
            return proj, aux, (proj, q_off), (aux, 0), (proj, f_off)
        return proj, aux, (proj, q_off), (proj, k_off), (aux, 0)

    @pl.when(s == 0)
    def _init():
        flag_ref[0] = 0
        flag_ref[1] = 0
        if has_s0:
            st_ref[...] = s0_ref[...]
        else:
            st_ref[...] = jnp.zeros(st_ref.shape, F32)

    ng = ng_ref[...]

    def projection_items(slot):
        proj, aux, _, _, _ = views(slot)
        xb = x_ref[...].reshape(rows, D_MODEL).astype(BF16)
        decay_max = []

        def cumsum_in_place(ref, cs):
            m = jnp.zeros((SUBLANES, cs.stop - cs.start), F32)
            for c in range(n_chunks):
                rs = slice(c * chunk, (c + 1) * chunk)
                b = _cumsum_rows(ref[rs, cs])
                ref[rs, cs] = b
                m = jnp.maximum(m, -b[0:SUBLANES])
                for j in range(1, nb):
                    m = jnp.maximum(m, b[SUBLANES * (j - 1):SUBLANES * j] - b[SUBLANES * j:SUBLANES * (j + 1)])
            decay_max.append(jnp.max(m))

        if kind == "hgrn":
            p = lb_ref[...]
            pe = jnp.exp(p - jnp.max(p, axis=0, keepdims=True))
            lb = jnp.sum(pe[:layer + 1], axis=0, keepdims=True) / jnp.sum(pe, axis=0, keepdims=True)

        def col_block(n0):
            cs = slice(n0, n0 + PROJ_COLS)
            blk = _dot(xb, w_in_ref[:, cs])
            if n0 >= g_off:
                proj[:, cs] = _silu(blk)
            elif n0 >= v_off:
                proj[:, cs] = blk
            elif kind == "hgrn" and n0 >= f_off:
                fs = slice(n0 - f_off, n0 - f_off + PROJ_COLS)
                f = lb[:, fs] + (1.0 - lb[:, fs]) * _sigmoid(blk)
                aux[:, fs] = 1.0 - f
                proj[:, cs] = jnp.log2(f)
                cumsum_in_place(proj, cs)
            elif kind == "hgrn":
                proj[:, cs] = _silu(blk)
            elif n0 >= k_off:
                proj[:, cs] = blk
            else:
                proj[:, cs] = blk * (dk ** -0.5)

        def gla_gate():
            r = _dot(xb, w_r_ref[...]).astype(BF16)
            z = _dot(r, w_gk2_ref[...]) + b_gk2_ref[...]
            aux[...] = _log2_sigmoid(z) * (1.0 / GLA_GATE_NORMALIZER)
            cumsum_in_place(aux, slice(0, kdim))

        def record_bound():
            worst = functools.reduce(jnp.maximum, decay_max)
            single_max, blocks_max = _mode_limits(chunk)
            flag_ref[slot] = jnp.where(
                worst <= single_max, MODE_SINGLE,
                jnp.where(worst <= blocks_max, MODE_BLOCKS, MODE_PAIRWISE)).astype(jnp.int32)

        items = [functools.partial(col_block, n0) for n0 in range(0, ncols, PROJ_COLS)]
        if kind == "gla":
            items.append(gla_gate)
        return items, record_bound

    def finish_chunk(proj, c_rows, b_idx, finishers):
        for h, finish in enumerate(finishers):
            o, st_new = finish(st_ref[b_idx, h])
            st_ref[b_idx, h] = st_new
            gate = proj[c_rows, g_off + h * dv:g_off + (h + 1) * dv]
            o_ref[c_rows, h * dv:(h + 1) * dv] = _rms_gate(o, ng, gate).astype(BF16)

    def attention_items(slot):
        proj, _, q_view, k_view, b_view = views(slot)
        items, staged = [], {}

        def issue(chunks):
            for c in chunks:
                r0 = c * chunk
                staged[c] = [_head_attention(q_view, k_view, b_view, None, None, r0, h, dk,
                                             proj[r0:r0 + chunk, v_off + h * dv:v_off + (h + 1) * dv],
                                             chunk, MODE_SINGLE) for h in range(heads)]

        def consume(chunks):
            for c in chunks:
                finish_chunk(proj, slice(c * chunk, (c + 1) * chunk), c // nch, staged.pop(c))

        for c0 in range(0, n_chunks, per_item):
            chunks = list(range(c0, c0 + per_item))
            items += [functools.partial(issue, chunks), functools.partial(consume, chunks)]
        return items

    def attention_loop(slot, mode):
        exact_diag = mode == MODE_PAIRWISE
        proj, _, q_view, k_view, b_view = views(slot)

        def first_stage(ci, copy):
            r0 = pl.multiple_of(ci * chunk, chunk)
            rs = pl.ds(r0, chunk)
            bc, kc = bc_ref.at[copy], kc_ref.at[copy]
            bc[...] = b_view[0][rs, b_view[1]:b_view[1] + kdim]
            if exact_diag:
                kc[...] = k_view[0][rs, k_view[1]:k_view[1] + kdim]
            return [_head_attention(q_view, k_view, b_view, bc, kc, r0, h, dk,
                                    proj[rs, v_off + h * dv:v_off + (h + 1) * dv],
                                    chunk, mode) for h in range(heads)]

        per_trip = 1 if exact_diag else per_item

        def trip(ti, carry):
            staged = [first_stage(ti * per_trip + j, j) for j in range(per_trip)]
            for j in range(per_trip):
                ci = ti * per_trip + j
                finish_chunk(proj, pl.ds(pl.multiple_of(ci * chunk, chunk), chunk), ci // nch, staged[j])
            return carry
        lax.fori_loop(0, n_chunks // per_trip, trip, 0)

    prev_mode = flag_ref[prv]
    overlap = has_cur & has_prev & (prev_mode == MODE_SINGLE)

    def output_matmul():
        return _dot(o_ref[...], w_out_ref[...])

    def output_norm(y):
        xp = xp_ref[...].reshape(rows, D_MODEL)
        out = _layer_norm(ALPHA * xp + y, lng_ref[...], lnb_ref[...])
        y_ref[...] = out.reshape(y_ref.shape)

    def project_and_attend(slot):
        proj_items, record_bound = projection_items(slot)
        attn_items = attention_items(1 - slot)
        held_back = proj_items[len(proj_items) - OUTPUT_OVERLAP_ITEMS:]
        proj_items = proj_items[:len(proj_items) - OUTPUT_OVERLAP_ITEMS]
        done = 0
        for i, attend in enumerate(attn_items):
            upto = -(-len(proj_items) * (i + 1) // len(attn_items))
            for item in proj_items[done:upto]:
                item()
            done = upto
            attend()
        y = output_matmul()
        for item in held_back[:-1]:
            item()
        output_norm(y)
        held_back[-1]()
        record_bound()

    for slot in range(2):
        pl.when(overlap & (cur == slot))(functools.partial(project_and_attend, slot))

    @pl.when(jnp.logical_not(overlap))
    def _one_at_a_time():
        for mode in (MODE_SINGLE, MODE_BLOCKS, MODE_PAIRWISE):
            pl.when(has_prev & (prev_mode == mode))(functools.partial(attention_loop, prv, mode))

        @pl.when(has_cur)
        def _project():
            proj_items, record_bound = projection_items(cur)
            for item in proj_items:
                item()
            record_bound()

    @pl.when(has_prev & jnp.logical_not(overlap))
    def _output():
        output_norm(output_matmul())

    @pl.when(s == ns - 1)
    def _write_state():
        so_ref[...] = st_ref[...]


def _weight_operand(w):
    if isinstance(w, tuple):
        arr, layer = w
        nd = arr.ndim - 1
        return arr, pl.BlockSpec((None,) + arr.shape[1:], lambda b, s: (layer,) + (0,) * nd,
                                 pipeline_mode=pl.Buffered(1))
    nd = w.ndim
    return w, pl.BlockSpec(w.shape, lambda b, s: (0,) * nd, pipeline_mode=pl.Buffered(1))


def _mixer_call(kind, layer, x, s0, weights, *, bb, rows_t, chunk):
    B, T, D = x.shape
    heads, dk, dv = (HGRN_HEADS, HGRN_DK, HGRN_DV) if kind == "hgrn" else (GLA_HEADS, GLA_DK, GLA_DV)
    nch = rows_t // chunk
    rows = bb * rows_t
    nt = T // rows_t
    grid = (B // bb, nt + 1)
    has_s0 = s0 is not None

    def this_block(b, s):
        return (b, jnp.minimum(s, nt - 1), 0)

    def prev_block(b, s):
        return (b, jnp.maximum(s - 1, 0), 0)

    in_specs = [pl.BlockSpec((bb, rows_t, D), this_block),
                pl.BlockSpec((bb, rows_t, D), prev_block)]
    args = [x, x]
    if has_s0:
        in_specs.append(pl.BlockSpec((bb, heads, dk, dv), lambda b, s: (b, 0, 0, 0)))
        args.append(s0)
    for w in weights:
        arr, spec = _weight_operand(w)
        in_specs.append(spec)
        args.append(arr)

    ncols = 3 * HGRN_FDIM + D_MODEL if kind == "hgrn" else 2 * GLA_KEY_DIM + 2 * GLA_VAL_DIM
    scratch = [
        pltpu.VMEM((2, rows, ncols), F32),
        pltpu.VMEM((2, rows, heads * dk), F32),
        pltpu.VMEM((_chunks_per_trip(chunk, bb * nch), chunk, heads * dk), F32),
        pltpu.VMEM((_chunks_per_trip(chunk, bb * nch), chunk, heads * dk), F32),
        pltpu.VMEM((rows, heads * dv), BF16),
        pltpu.VMEM((bb, heads, dk, dv), F32),
        pltpu.SMEM((2,), jnp.int32),
    ]
    kern = functools.partial(_mixer_kernel, kind=kind, layer=layer, bb=bb, nch=nch, chunk=chunk,
                             has_s0=has_s0)
    return pl.pallas_call(
        kern,
        grid=grid,
        in_specs=in_specs,
        out_specs=[pl.BlockSpec((bb, rows_t, D), prev_block),
                   pl.BlockSpec((bb, heads, dk, dv), lambda b, s: (b, 0, 0, 0))],
        out_shape=[jax.ShapeDtypeStruct((B, T, D), F32),
                   jax.ShapeDtypeStruct((B, heads, dk, dv), F32)],
        scratch_shapes=scratch,
        compiler_params=pltpu.CompilerParams(
            dimension_semantics=("arbitrary", "arbitrary"),
            vmem_limit_bytes=VMEM_LIMIT_BYTES),
        name=f"{kind}_mixer_{'s' if has_s0 else 'p'}",
    )(*args)


def _ffn_kernel(*refs, bb, rows_t, has_s0):
    it = iter(refs)
    x_ref = next(it)
    xp_ref = next(it)
    c0_ref = next(it) if has_s0 else None
    w_up_ref = next(it)
    cw_ref = next(it)
    cb_ref = next(it)
    w_dn_ref = next(it)
    lng_ref = next(it)
    lnb_ref = next(it)
    y_ref = next(it)
    co_ref = next(it)
    a_ref = next(it)
    h_ref = next(it)
    acc_ref = next(it)

    rows = bb * rows_t
    t = pl.program_id(1)
    nt = pl.num_programs(1)
    pad = SUBLANES

    @pl.when(t == 0)
    def _init():
        acc_ref[...] = jnp.zeros(acc_ref.shape, F32)
        if has_s0:
            a_ref[:, pad - 2:pad, :] = c0_ref[...]
        else:
            a_ref[:, pad - 2:pad, :] = jnp.zeros((bb, 2, D_FF), F32)

    def finish_previous():
        xp = xp_ref[...].reshape(rows, D_MODEL)
        out = _layer_norm(ALPHA * xp + acc_ref[...], lng_ref[...], lnb_ref[...])
        y_ref[...] = out.reshape(y_ref.shape)

    @pl.when(t == nt - 1)
    def _drain():
        finish_previous()

    def hidden_blocks(starts):
        xb = x_ref[...].reshape(rows, D_MODEL).astype(BF16)

        def up_proj(n0):
            cs = slice(n0, n0 + FFN_COLS)
            a = _dot(xb, w_up_ref[:, cs])
            a_ref[:, pad:pad + rows_t, cs] = a.reshape(bb, rows_t, FFN_COLS)
            return _dot(xb, w_up_ref[:, D_FF + n0:D_FF + n0 + FFN_COLS])

        def activate(n0, g):
            cs = slice(n0, n0 + FFN_COLS)
            conv = (cb_ref[:, cs]
                    + a_ref[:, pad - 2:pad - 2 + rows_t, cs] * cw_ref[0:1, cs]
                    + a_ref[:, pad - 1:pad - 1 + rows_t, cs] * cw_ref[1:2, cs]
                    + a_ref[:, pad:pad + rows_t, cs] * cw_ref[2:3, cs])
            h_ref[:, cs] = (_gelu_tanh(conv).reshape(rows, FFN_COLS) * g).astype(BF16)

        g_next = up_proj(starts[0])
        for idx, n0 in enumerate(starts):
            g = g_next
            if idx + 1 < len(starts):
                g_next = up_proj(starts[idx + 1])
            activate(n0, g)

    @pl.when(t < nt - 1)
    def _block():
        finish_previous()
        hidden_blocks(list(range(0, D_FF, FFN_COLS)))

        for n0 in range(0, D_MODEL, DOWN_COLS):
            acc_ref[:, n0:n0 + DOWN_COLS] = _dot(h_ref[...], w_dn_ref[:, n0:n0 + DOWN_COLS])

        last2 = a_ref[:, pad + rows_t - 2:pad + rows_t, :]
        a_ref[:, pad - 2:pad, :] = last2

        @pl.when(t == nt - 2)
        def _write_carry():
            co_ref[...] = last2


def _ffn_call(x, c0, weights, *, bb, rows_t):
    B, T, D = x.shape
    nt = T // rows_t
    grid = (B // bb, nt + 1)
    has_s0 = c0 is not None

    def this_block(b, t):
        return (b, jnp.minimum(t, nt - 1), 0)

    def prev_block(b, t):
        return (b, jnp.maximum(t - 1, 0), 0)

    in_specs = [pl.BlockSpec((bb, rows_t, D), this_block),
                pl.BlockSpec((bb, rows_t, D), prev_block)]
    args = [x, x]
    if has_s0:
        in_specs.append(pl.BlockSpec((bb, CONV_WIDTH - 1, D_FF), lambda b, t: (b, 0, 0)))
        args.append(c0)
    for w in weights:
        arr, spec = _weight_operand(w)
        in_specs.append(spec)
        args.append(arr)
    kern = functools.partial(_ffn_kernel, bb=bb, rows_t=rows_t, has_s0=has_s0)
    return pl.pallas_call(
        kern,
        grid=grid,
        in_specs=in_specs,
        out_specs=[pl.BlockSpec((bb, rows_t, D), prev_block),
                   pl.BlockSpec((bb, CONV_WIDTH - 1, D_FF), lambda b, t: (b, 0, 0))],
        out_shape=[jax.ShapeDtypeStruct((B, T, D), F32),
                   jax.ShapeDtypeStruct((B, CONV_WIDTH - 1, D_FF), F32)],
        scratch_shapes=[pltpu.VMEM((bb, SUBLANES + rows_t, D_FF), F32),
                        pltpu.VMEM((bb * rows_t, D_FF), BF16),
                        pltpu.VMEM((bb * rows_t, D), F32)],
        compiler_params=pltpu.CompilerParams(
            dimension_semantics=("arbitrary", "arbitrary"),
            vmem_limit_bytes=VMEM_LIMIT_BYTES),
        name=f"conv_ffn_{'s' if has_s0 else 'p'}",
    )(*args)


def _trunk(x, st_hgrn, st_gla, st_conv, layer_weights, *, mixer_bb, ffn_bb, rows_t, chunk):
    new_h, new_g, new_c = [], [], []
    for i in range(DEPTH):
        kind, mix_w, ffn_w = layer_weights[i]
        j = i // 2
        if kind == "hgrn":
            s0 = None if st_hgrn is None else st_hgrn[j]
        else:
            s0 = None if st_gla is None else st_gla[j]
        x, s = _mixer_call(kind, i, x, s0, mix_w, bb=mixer_bb, rows_t=rows_t, chunk=chunk)
        (new_h if kind == "hgrn" else new_g).append(s)
        c0 = None if st_conv is None else st_conv[i]
        x, c = _ffn_call(x, c0, ffn_w, bb=ffn_bb, rows_t=rows_t)
        new_c.append(c)
    return x, jnp.stack(new_h), jnp.stack(new_g), jnp.stack(new_c)


def kernel(x_prompt, x_sample, state_hgrn, state_gla, state_ffn_conv, lb_param, hgrn_w_in, hgrn_w_out, hgrn_norm_g, gla_w_in, gla_w_gk2, gla_b_gk2, gla_w_out, gla_norm_g, ln_mix_g, ln_mix_b, ffn_w_up, ffn_conv_w, ffn_conv_b, ffn_w_down, ln_ffn_g, ln_ffn_b):
    gla_main = 2 * GLA_KEY_DIM + 2 * GLA_VAL_DIM
    hgrn_w_in_b, hgrn_w_out_b = hgrn_w_in.astype(BF16), hgrn_w_out.astype(BF16)
    gla_w_in_b, gla_w_out_b = gla_w_in[:, :, :gla_main].astype(BF16), gla_w_out.astype(BF16)
    ffn_w_up_b, ffn_w_down_b = ffn_w_up.astype(BF16), ffn_w_down.astype(BF16)
    layer_weights = []
    for i in range(DEPTH):
        j = i // 2
        if i % 2 == 0:
            mix_w = (lb_param.astype(F32),
                     (hgrn_w_in_b, j),
                     (hgrn_w_out_b, j),
                     hgrn_norm_g[j].reshape(1, HGRN_DV),
                     ln_mix_g[i].reshape(1, D_MODEL),
                     ln_mix_b[i].reshape(1, D_MODEL))
            kind = "hgrn"
        else:
            w_r = jnp.pad(gla_w_in[j][:, gla_main:], ((0, 0), (0, LANES - GLA_GATE_RANK)))
            w_gk2 = jnp.pad(gla_w_gk2[j], ((0, LANES - GLA_GATE_RANK), (0, 0)))
            mix_w = ((gla_w_in_b, j),
                     w_r.astype(BF16),
                     w_gk2.astype(BF16),
                     gla_b_gk2[j].reshape(1, GLA_KEY_DIM),
                     (gla_w_out_b, j),
                     gla_norm_g[j].reshape(1, GLA_DV),
                     ln_mix_g[i].reshape(1, D_MODEL),
                     ln_mix_b[i].reshape(1, D_MODEL))
            kind = "gla"
        ffn_w = ((ffn_w_up_b, i),
                 ffn_conv_w[i],
                 ffn_conv_b[i].reshape(1, D_FF),
                 (ffn_w_down_b, i),
                 ln_ffn_g[i].reshape(1, D_MODEL),
                 ln_ffn_b[i].reshape(1, D_MODEL))
        layer_weights.append((kind, mix_w, ffn_w))

    y_p, h_p, g_p, c_p = _trunk(x_prompt, None, None, None, layer_weights,
                                mixer_bb=1, ffn_bb=1, rows_t=PROMPT_ROWS, chunk=CHUNK)
    dec_b, dec_t = x_sample.shape[0], x_sample.shape[1]
    y_s, h_s, g_s, c_s = _trunk(x_sample, state_hgrn, state_gla, state_ffn_conv, layer_weights,
                                mixer_bb=8, ffn_bb=dec_b, rows_t=dec_t, chunk=dec_t)
    return (y_p, y_s, h_p, h_s, g_p, g_s, c_p, c_s)
```

```python
import functools
import math

import jax
import jax.numpy as jnp
from jax import lax
from jax.experimental import pallas as pl
from jax.experimental.pallas import tpu as pltpu

F32 = jnp.float32
BF16 = jnp.bfloat16

D_MODEL = 1024
DEPTH = 2
CHUNK = 64
HGRN_HEADS = 8
HGRN_DK = 128
HGRN_DV = 128
HGRN_FDIM = HGRN_HEADS * HGRN_DK
GLA_HEADS = 4
GLA_KEY_DIM = 512
GLA_VAL_DIM = 1024
GLA_DK = 128
GLA_DV = 256
GLA_GATE_RANK = 16
GLA_GATE_NORMALIZER = 16.0
D_FF = 2816
CONV_WIDTH = 3
ALPHA = (2.0 * DEPTH) ** 0.25
LN_EPS = 1e-5
RMS_EPS = 1e-6
LOG2E = 1.4426950408889634

SUBLANES = 8
LANES = 128
VMEM_LIMIT_BYTES = 56 * 1024 * 1024
PROMPT_ROWS = 512
FFN_COLS = 256
DOWN_COLS = 512
PROJ_COLS = 512
MAX_FACTOR_LOG2 = 100.0
FAST_BLOCK_ROWS = 2 * SUBLANES
MODE_SINGLE, MODE_BLOCKS, MODE_PAIRWISE = 2, 1, 0
TRIP_ROWS = 128


def _chunks_per_trip(chunk, n_chunks):
    return min(n_chunks, max(TRIP_ROWS // chunk, 1))


OUTPUT_OVERLAP_ITEMS = 2


def _dot(a, b):
    return jnp.dot(a, b, preferred_element_type=F32)


def _dot_nt(a, b):
    return lax.dot_general(a, b, (((1,), (1,)), ((), ())), preferred_element_type=F32)


def _dot_tn(a, b):
    return lax.dot_general(a, b, (((0,), (0,)), ((), ())), preferred_element_type=F32)


def _sigmoid(x):
    return 1.0 / (1.0 + jnp.exp(-x))


def _silu(x):
    return x * _sigmoid(x)


def _gelu_tanh(x):
    c = math.sqrt(2.0 / math.pi)
    k = -2.0 * c * LOG2E
    z = x * ((k * 0.044715) * (x * x) + k)
    return x / (1.0 + jnp.exp2(z))


def _log2_sigmoid(x):
    return jnp.minimum(x, 0.0) * LOG2E - jnp.log2(1.0 + jnp.exp2(jnp.abs(x) * (-LOG2E)))


def _layer_norm(x, g, b):
    mu = jnp.mean(x, axis=-1, keepdims=True)
    xc = x - mu
    var = jnp.mean(xc * xc, axis=-1, keepdims=True)
    return xc * lax.rsqrt(var + LN_EPS) * g + b


def _cumsum_rows(g):
    c = g.shape[0]
    r = lax.broadcasted_iota(jnp.int32, (c, c), 0)
    s = lax.broadcasted_iota(jnp.int32, (c, c), 1)
    tri = (s <= r).astype(BF16)
    hi = g.astype(BF16)
    lo = (g - hi.astype(F32)).astype(BF16)
    return _dot(tri, hi) + _dot(tri, lo)


def _mode_limits(chunk):
    half_windows = max(chunk // 2 // SUBLANES, 1)
    block_windows = min(FAST_BLOCK_ROWS, chunk) // SUBLANES
    return MAX_FACTOR_LOG2 / half_windows, MAX_FACTOR_LOG2 / block_windows


def _head_attention(q_view, k_view, b_view, bc_ref, kc_ref, r0, h, dk, v, chunk, mode):
    exact_diag = mode == MODE_PAIRWISE
    if mode == MODE_PAIRWISE:
        R = SUBLANES
    elif mode == MODE_BLOCKS:
        R = min(FAST_BLOCK_ROWS, chunk)
    else:
        R = chunk
    ref_in_block = R // 2 - 1 if mode == MODE_SINGLE else R - 1
    nb = chunk // R
    static_rows = isinstance(r0, int)

    def cols(view):
        return slice(view[1] + h * dk, view[1] + (h + 1) * dk)

    def blk(view, i):
        if static_rows:
            return view[0][r0 + R * i:r0 + R * (i + 1), cols(view)]
        return view[0][pl.ds(pl.multiple_of(r0 + R * i, R), R), cols(view)]

    def row(view, copy_ref, t):
        if static_rows:
            one = view[0][r0 + t:r0 + t + 1, cols(view)]
        else:
            one = copy_ref[t:t + 1, h * dk:(h + 1) * dk]
        return jnp.broadcast_to(one, (R, dk))

    def full(view):
        if static_rows:
            return view[0][r0:r0 + chunk, cols(view)]
        return view[0][pl.ds(r0, chunk), cols(view)]

    def b_end(j):
        return row(b_view, bc_ref, R * j + ref_in_block)

    kpp = jnp.concatenate(
        [blk(k_view, j) * jnp.exp2(b_end(j) - blk(b_view, j)) for j in range(nb)],
        axis=0).astype(BF16)

    col = lax.broadcasted_iota(jnp.int32, (R, chunk), 1)
    rw = lax.broadcasted_iota(jnp.int32, (R, chunk), 0)
    colblk = col // R

    lq, pair = [], {}
    for i in range(nb):
        qi, bi = blk(q_view, i), blk(b_view, i)
        for j in range(i if exact_diag else i + 1):
            pair[(i, j)] = len(lq)
            lq.append(qi * jnp.exp2(bi - b_end(j)))
    if lq:
        cross = _dot_nt(jnp.concatenate(lq, axis=0).astype(BF16), kpp)

    def cross_blk(i, j):
        p = pair[(i, j)]
        return cross[R * p:R * (p + 1), :]

    b_all = full(b_view)
    if static_rows:
        b_last = b_view[0][r0 + chunk - 1:r0 + chunk, cols(b_view)]
    else:
        b_last = bc_ref[chunk - 1:chunk, h * dk:(h + 1) * dk]
    vb = v.astype(BF16)
    qe = (full(q_view) * jnp.exp2(b_all)).astype(BF16)
    kd = (full(k_view) * jnp.exp2(b_last - b_all)).astype(BF16)
    kv = _dot_tn(kd, vb)
    dv = v.shape[1]
    decay = jnp.broadcast_to(jnp.exp2(b_last), (dk, dk)).T
    if dv > dk:
        decay = jnp.concatenate([decay] * (dv // dk), axis=1)

    def finish(st):
        a_rows = []
        for i in range(nb):
            if exact_diag:
                acc = jnp.zeros((R, chunk), F32)
                for j in range(i):
                    acc = jnp.where(colblk == j, cross_blk(i, j), acc)
                qi, bi = blk(q_view, i), blk(b_view, i)
                for s in range(R):
                    t = R * i + s
                    e = jnp.exp2(jnp.minimum(bi - row(b_view, bc_ref, t), 0.0))
                    cs = jnp.sum(qi * e * row(k_view, kc_ref, t), axis=-1, keepdims=True)
                    acc = jnp.where(col == t, cs, acc)
            else:
                acc = cross_blk(i, 0)
                for j in range(1, i + 1):
                    acc = jnp.where(colblk == j, cross_blk(i, j), acc)
            a_rows.append(jnp.where(col <= R * i + rw, acc, 0.0))
        a = jnp.concatenate(a_rows, axis=0).astype(BF16)
        o = _dot(jnp.concatenate([qe, a], axis=1),
                 jnp.concatenate([st.astype(BF16), vb], axis=0))
        st_new = decay * st + kv
        return o, st_new

    return finish


def _rms_gate(o, norm_g, gate_act):
    ms = jnp.mean(o * o, axis=-1, keepdims=True)
    return o * lax.rsqrt(ms + RMS_EPS) * norm_g * gate_act


def _mixer_kernel(*refs, kind, layer, bb, nch, chunk, nt, has_s0):
    it = iter(refs)
    x_ref = next(it)
    xp_ref = next(it)
    s0_ref = next(it) if has_s0 else None
    if kind == "hgrn":
        lb_ref = next(it)
        w_in_ref = next(it)
    else:
        w_in_ref = next(it)
        w_r_ref = next(it)
        w_gk2_ref = next(it)
        b_gk2_ref = next(it)
    w_out_ref = next(it)
    ng_ref = next(it)
    lng_ref = next(it)
    lnb_ref = next(it)
    y_ref = next(it)
    so_ref = next(it)
    proj2_ref = next(it)
    aux2_ref = next(it)
    bc_ref = next(it)
    kc_ref = next(it)
    o_ref = next(it)
    st_ref = next(it)
    flag_ref = next(it)

    heads, dk, dv = (HGRN_HEADS, HGRN_DK, HGRN_DV) if kind == "hgrn" else (GLA_HEADS, GLA_DK, GLA_DV)
    kdim = heads * dk
    rows = bb * nch * chunk
    n_chunks = bb * nch
    per_item = _chunks_per_trip(chunk, n_chunks)
    nb = chunk // SUBLANES
    ncols = proj2_ref.shape[2]
    s = pl.program_id(1)
    ns = pl.num_programs(1)
    cur = s % 2
    prv = 1 - cur
    has_cur = s < ns - 1
    has_prev = s > 0
    prev_t = (s - 1) % nt

    if kind == "hgrn":
        q_off, f_off, v_off, g_off = 0, kdim, 2 * kdim, 3 * kdim
    else:
        q_off, k_off, v_off, g_off = 0, kdim, 2 * kdim, 2 * kdim + heads * dv

    def views(slot):
        proj, aux = proj2_ref.at[slot], aux2_ref.at[slot]
        if kind == "hgrn":
            return proj, aux, (proj, q_off), (aux, 0), (proj, f_off)
        return proj, aux, (proj, q_off), (proj, k_off), (aux, 0)

    @pl.when(s == 0)
    def _init_flags():
        flag_ref[0] = 0
        flag_ref[1] = 0

    @pl.when(has_prev & (prev_t == 0))
    def _init_state():
        if has_s0:
            st_ref[...] = s0_ref[...]
        else:
            st_ref[...] = jnp.zeros(st_ref.shape, F32)

    ng = ng_ref[...]

    def projection_items(slot):
        proj, aux, _, _, _ = views(slot)
        xb = x_ref[...].reshape(rows, D_MODEL).astype(BF16)
        decay_max = []

        def cumsum_in_place(ref, cs):
            m = jnp.zeros((SUBLANES, cs.stop - cs.start), F32)
            for c in range(n_chunks):
                rs = slice(c * chunk, (c + 1) * chunk)
                b = _cumsum_rows(ref[rs, cs])
                ref[rs, cs] = b
                m = jnp.maximum(m, -b[0:SUBLANES])
                for j in range(1, nb):
                    m = jnp.maximum(m, b[SUBLANES * (j - 1):SUBLANES * j] - b[SUBLANES * j:SUBLANES * (j + 1)])
            decay_max.append(jnp.max(m))

        if kind == "hgrn":
            p = lb_ref[...]
            pe = jnp.exp(p - jnp.max(p, axis=0, keepdims=True))
            lb = jnp.sum(pe[:layer + 1], axis=0, keepdims=True) / jnp.sum(pe, axis=0, keepdims=True)

        def col_block(n0):
            cs = slice(n0, n0 + PROJ_COLS)
            blk = _dot(xb, w_in_ref[:, cs])
            if n0 >= g_off:
                proj[:, cs] = _silu(blk)
            elif n0 >= v_off:
                proj[:, cs] = blk
            elif kind == "hgrn" and n0 >= f_off:
                fs = slice(n0 - f_off, n0 - f_off + PROJ_COLS)
                f = lb[:, fs] + (1.0 - lb[:, fs]) * _sigmoid(blk)
                aux[:, fs] = 1.0 - f
                proj[:, cs] = jnp.log2(f)
                cumsum_in_place(proj, cs)
            elif kind == "hgrn":
                proj[:, cs] = _silu(blk)
            elif n0 >= k_off:
                proj[:, cs] = blk
            else:
                proj[:, cs] = blk * (dk ** -0.5)

        def gla_gate():
            r = _dot(xb, w_r_ref[...]).astype(BF16)
            z = _dot(r, w_gk2_ref[...]) + b_gk2_ref[...]
            aux[...] = _log2_sigmoid(z) * (1.0 / GLA_GATE_NORMALIZER)
            cumsum_in_place(aux, slice(0, kdim))

        def record_bound():
            worst = functools.reduce(jnp.maximum, decay_max)
            single_max, blocks_max = _mode_limits(chunk)
            flag_ref[slot] = jnp.where(
                worst <= single_max, MODE_SINGLE,
                jnp.where(worst <= blocks_max, MODE_BLOCKS, MODE_PAIRWISE)).astype(jnp.int32)

        items = [functools.partial(col_block, n0) for n0 in range(0, ncols, PROJ_COLS)]
        if kind == "gla":
            items.append(gla_gate)
        return items, record_bound

    def finish_chunk(proj, c_rows, b_idx, finishers):
        for h, finish in enumerate(finishers):
            o, st_new = finish(st_ref[b_idx, h])
            st_ref[b_idx, h] = st_new
            gate = proj[c_rows, g_off + h * dv:g_off + (h + 1) * dv]
            o_ref[c_rows, h * dv:(h + 1) * dv] = _rms_gate(o, ng, gate).astype(BF16)

    def attention_items(slot):
        proj, _, q_view, k_view, b_view = views(slot)
        items, staged = [], {}

        def issue(chunks):
            for c in chunks:
                r0 = c * chunk
                staged[c] = [_head_attention(q_view, k_view, b_view, None, None, r0, h, dk,
                                             proj[r0:r0 + chunk, v_off + h * dv:v_off + (h + 1) * dv],
                                             chunk, MODE_SINGLE) for h in range(heads)]

        def consume(chunks):
            for c in chunks:
                finish_chunk(proj, slice(c * chunk, (c + 1) * chunk), c // nch, staged.pop(c))

        for c0 in range(0, n_chunks, per_item):
            chunks = list(range(c0, c0 + per_item))
            items += [functools.partial(issue, chunks), functools.partial(consume, chunks)]
        return items

    def attention_loop(slot, mode):
        exact_diag = mode == MODE_PAIRWISE
        proj, _, q_view, k_view, b_view = views(slot)

        def first_stage(ci, copy):
            r0 = pl.multiple_of(ci * chunk, chunk)
            rs = pl.ds(r0, chunk)
            bc, kc = bc_ref.at[copy], kc_ref.at[copy]
            bc[...] = b_view[0][rs, b_view[1]:b_view[1] + kdim]
            if exact_diag:
                kc[...] = k_view[0][rs, k_view[1]:k_view[1] + kdim]
            return [_head_attention(q_view, k_view, b_view, bc, kc, r0, h, dk,
                                    proj[rs, v_off + h * dv:v_off + (h + 1) * dv],
                                    chunk, mode) for h in range(heads)]

        per_trip = 1 if exact_diag else per_item

        def trip(ti, carry):
            staged = [first_stage(ti * per_trip + j, j) for j in range(per_trip)]
            for j in range(per_trip):
                ci = ti * per_trip + j
                finish_chunk(proj, pl.ds(pl.multiple_of(ci * chunk, chunk), chunk), ci // nch, staged[j])
            return carry
        lax.fori_loop(0, n_chunks // per_trip, trip, 0)

    prev_mode = flag_ref[prv]
    overlap = has_cur & has_prev & (prev_mode == MODE_SINGLE)

    def output_matmul():
        return _dot(o_ref[...], w_out_ref[...])

    def output_norm(y):
        xp = xp_ref[...].reshape(rows, D_MODEL)
        out = _layer_norm(ALPHA * xp + y, lng_ref[...], lnb_ref[...])
        y_ref[...] = out.reshape(y_ref.shape)

    def project_and_attend(slot):
        proj_items, record_bound = projection_items(slot)
        attn_items = attention_items(1 - slot)
        held_back = proj_items[len(proj_items) - OUTPUT_OVERLAP_ITEMS:]
        proj_items = proj_items[:len(proj_items) - OUTPUT_OVERLAP_ITEMS]
        done = 0
        for i, attend in enumerate(attn_items):
            upto = -(-len(proj_items) * (i + 1) // len(attn_items))
            for item in proj_items[done:upto]:
                item()
            done = upto
            attend()
        y = output_matmul()
        for item in held_back[:-1]:
            item()
        output_norm(y)
        held_back[-1]()
        record_bound()

    for slot in range(2):
        pl.when(overlap & (cur == slot))(functools.partial(project_and_attend, slot))

    @pl.when(jnp.logical_not(overlap))
    def _one_at_a_time():
        for mode in (MODE_SINGLE, MODE_BLOCKS, MODE_PAIRWISE):
            pl.when(has_prev & (prev_mode == mode))(functools.partial(attention_loop, prv, mode))

        @pl.when(has_cur)
        def _project():
            proj_items, record_bound = projection_items(cur)
            for item in proj_items:
                item()
            record_bound()

    @pl.when(has_prev & jnp.logical_not(overlap))
    def _output():
        output_norm(output_matmul())

    @pl.when(has_prev & (prev_t == nt - 1))
    def _write_state():
        so_ref[...] = st_ref[...]


def _weight_operand(w):
    if isinstance(w, tuple):
        arr, layer = w
        nd = arr.ndim - 1
        return arr, pl.BlockSpec((None,) + arr.shape[1:], lambda b, s: (layer,) + (0,) * nd,
                                 pipeline_mode=pl.Buffered(1))
    nd = w.ndim
    return w, pl.BlockSpec(w.shape, lambda b, s: (0,) * nd, pipeline_mode=pl.Buffered(1))


def _mixer_call(kind, layer, x, s0, weights, *, bb, rows_t, chunk):
    B, T, D = x.shape
    heads, dk, dv = (HGRN_HEADS, HGRN_DK, HGRN_DV) if kind == "hgrn" else (GLA_HEADS, GLA_DK, GLA_DV)
    nch = rows_t // chunk
    rows = bb * rows_t
    nt = T // rows_t
    n_blocks = (B // bb) * nt
    grid = (1, n_blocks + 1)
    has_s0 = s0 is not None

    def this_block(_, s):
        blk = jnp.minimum(s, n_blocks - 1)
        return (blk // nt, blk % nt, 0)

    def prev_block(_, s):
        blk = jnp.maximum(s - 1, 0)
        return (blk // nt, blk % nt, 0)

    def prev_group(_, s):
        return (jnp.maximum(s - 1, 0) // nt, 0, 0, 0)

    in_specs = [pl.BlockSpec((bb, rows_t, D), this_block),
                pl.BlockSpec((bb, rows_t, D), prev_block)]
    args = [x, x]
    if has_s0:
        in_specs.append(pl.BlockSpec((bb, heads, dk, dv), prev_group))
        args.append(s0)
    for w in weights:
        arr, spec = _weight_operand(w)
        in_specs.append(spec)
        args.append(arr)

    ncols = 3 * HGRN_FDIM + D_MODEL if kind == "hgrn" else 2 * GLA_KEY_DIM + 2 * GLA_VAL_DIM
    scratch = [
        pltpu.VMEM((2, rows, ncols), F32),
        pltpu.VMEM((2, rows, heads * dk), F32),
        pltpu.VMEM((_chunks_per_trip(chunk, bb * nch), chunk, heads * dk), F32),
        pltpu.VMEM((_chunks_per_trip(chunk, bb * nch), chunk, heads * dk), F32),
        pltpu.VMEM((rows, heads * dv), BF16),
        pltpu.VMEM((bb, heads, dk, dv), F32),
        pltpu.SMEM((2,), jnp.int32),
    ]
    kern = functools.partial(_mixer_kernel, kind=kind, layer=layer, bb=bb, nch=nch, chunk=chunk,
                             nt=nt, has_s0=has_s0)
    return pl.pallas_call(
        kern,
        grid=grid,
        in_specs=in_specs,
        out_specs=[pl.BlockSpec((bb, rows_t, D), prev_block),
                   pl.BlockSpec((bb, heads, dk, dv), prev_group)],
        out_shape=[jax.ShapeDtypeStruct((B, T, D), F32),
                   jax.ShapeDtypeStruct((B, heads, dk, dv), F32)],
        scratch_shapes=scratch,
        compiler_params=pltpu.CompilerParams(
            dimension_semantics=("arbitrary", "arbitrary"),
            vmem_limit_bytes=VMEM_LIMIT_BYTES),
        name=f"{kind}_mixer_{'s' if has_s0 else 'p'}",
    )(*args)


def _ffn_kernel(*refs, bb, rows_t, has_s0):
    it = iter(refs)
    x_ref = next(it)
    c0_ref = next(it) if has_s0 else None
    w_up_ref = next(it)
    cw_ref = next(it)
    cb_ref = next(it)
    w_dn_ref = next(it)
    lng_ref = next(it)
    lnb_ref = next(it)
    y_ref = next(it)
    co_ref = next(it)
    a_ref = next(it)
    h_ref = next(it)
    acc_ref = next(it)

    rows = bb * rows_t
    t = pl.program_id(1)
    nt = pl.num_programs(1)
    pad = SUBLANES

    @pl.when(t == 0)
    def _init_carry():
        if has_s0:
            a_ref[:, pad - 2:pad, :] = c0_ref[...]
        else:
            a_ref[:, pad - 2:pad, :] = jnp.zeros((bb, 2, D_FF), F32)

    x = x_ref[...].reshape(rows, D_MODEL)
    xb = x.astype(BF16)

    def up_proj(n0):
        cs = slice(n0, n0 + FFN_COLS)
        a = _dot(xb, w_up_ref[:, cs])
        a_ref[:, pad:pad + rows_t, cs] = a.reshape(bb, rows_t, FFN_COLS)
        return _dot(xb, w_up_ref[:, D_FF + n0:D_FF + n0 + FFN_COLS])

    def activate(n0, g):
        cs = slice(n0, n0 + FFN_COLS)
        conv = (cb_ref[:, cs]
                + a_ref[:, pad - 2:pad - 2 + rows_t, cs] * cw_ref[0:1, cs]
                + a_ref[:, pad - 1:pad - 1 + rows_t, cs] * cw_ref[1:2, cs]
                + a_ref[:, pad:pad + rows_t, cs] * cw_ref[2:3, cs])
        h_ref[:, cs] = (_gelu_tanh(conv).reshape(rows, FFN_COLS) * g).astype(BF16)

    starts = list(range(0, D_FF, FFN_COLS))
    g_next = up_proj(starts[0])
    for idx, n0 in enumerate(starts):
        g = g_next
        if idx + 1 < len(starts):
            g_next = up_proj(starts[idx + 1])
        activate(n0, g)

    for n0 in range(0, D_MODEL, DOWN_COLS):
        acc_ref[:, n0:n0 + DOWN_COLS] = _dot(h_ref[...], w_dn_ref[:, n0:n0 + DOWN_COLS])

    out = _layer_norm(ALPHA * x + acc_ref[...], lng_ref[...], lnb_ref[...])
    y_ref[...] = out.reshape(y_ref.shape)

    last2 = a_ref[:, pad + rows_t - 2:pad + rows_t, :]
    a_ref[:, pad - 2:pad, :] = last2

    @pl.when(t == nt - 1)
    def _write_carry():
        co_ref[...] = last2


def _ffn_call(x, c0, weights, *, bb, rows_t):
    B, T, D = x.shape
    grid = (B // bb, T // rows_t)
    has_s0 = c0 is not None
    in_specs = [pl.BlockSpec((bb, rows_t, D), lambda b, t: (b, t, 0))]
    args = [x]
    if has_s0:
        in_specs.append(pl.BlockSpec((bb, CONV_WIDTH - 1, D_FF), lambda b, t: (b, 0, 0)))
        args.append(c0)
    for w in weights:
        arr, spec = _weight_operand(w)
        in_specs.append(spec)
        args.append(arr)
    kern = functools.partial(_ffn_kernel, bb=bb, rows_t=rows_t, has_s0=has_s0)
    return pl.pallas_call(
        kern,
        grid=grid,
        in_specs=in_specs,
        out_specs=[pl.BlockSpec((bb, rows_t, D), lambda b, t: (b, t, 0)),
                   pl.BlockSpec((bb, CONV_WIDTH - 1, D_FF), lambda b, t: (b, 0, 0))],
        out_shape=[jax.ShapeDtypeStruct((B, T, D), F32),
                   jax.ShapeDtypeStruct((B, CONV_WIDTH - 1, D_FF), F32)],
        scratch_shapes=[pltpu.VMEM((bb, SUBLANES + rows_t, D_FF), F32),
                        pltpu.VMEM((bb * rows_t, D_FF), BF16),
                        pltpu.VMEM((bb * rows_t, D), F32)],
        compiler_params=pltpu.CompilerParams(
            dimension_semantics=("arbitrary", "arbitrary"),
            vmem_limit_bytes=VMEM_LIMIT_BYTES),
        name=f"conv_ffn_{'s' if has_s0 else 'p'}",
    )(*args)


def _trunk(x, st_hgrn, st_gla, st_conv, layer_weights, *, mixer_bb, ffn_bb, rows_t, chunk):
    new_h, new_g, new_c = [], [], []
    for i in range(DEPTH):
        kind, mix_w, ffn_w = layer_weights[i]
        j = i // 2
        if kind == "hgrn":
            s0 = None if st_hgrn is None else st_hgrn[j]
        else:
            s0 = None if st_gla is None else st_gla[j]
        x, s = _mixer_call(kind, i, x, s0, mix_w, bb=mixer_bb, rows_t=rows_t, chunk=chunk)
        (new_h if kind == "hgrn" else new_g).append(s)
        c0 = None if st_conv is None else st_conv[i]
        x, c = _ffn_call(x, c0, ffn_w, bb=ffn_bb, rows_t=rows_t)
        new_c.append(c)
    return x, jnp.stack(new_h), jnp.stack(new_g), jnp.stack(new_c)


def kernel(x_prompt, x_sample, state_hgrn, state_gla, state_ffn_conv, lb_param, hgrn_w_in, hgrn_w_out, hgrn_norm_g, gla_w_in, gla_w_gk2, gla_b_gk2, gla_w_out, gla_norm_g, ln_mix_g, ln_mix_b, ffn_w_up, ffn_conv_w, ffn_conv_b, ffn_w_down, ln_ffn_g, ln_ffn_b):
    gla_main = 2 * GLA_KEY_DIM + 2 * GLA_VAL_DIM
    hgrn_w_in_b, hgrn_w_out_b = hgrn_w_in.astype(BF16), hgrn_w_out.astype(BF16)
    gla_w_in_b, gla_w_out_b = gla_w_in.astype(BF16), gla_w_out.astype(BF16)
    ffn_w_up_b, ffn_w_down_b = ffn_w_up.astype(BF16), ffn_w_down.astype(BF16)
    layer_weights = []
    for i in range(DEPTH):
        j = i // 2
        if i % 2 == 0:
            mix_w = (lb_param.astype(F32),
                     (hgrn_w_in_b, j),
                     (hgrn_w_out_b, j),
                     hgrn_norm_g[j].reshape(1, HGRN_DV),
                     ln_mix_g[i].reshape(1, D_MODEL),
                     ln_mix_b[i].reshape(1, D_MODEL))
            kind = "hgrn"
        else:
            w_r = jnp.pad(gla_w_in[j][:, gla_main:], ((0, 0), (0, LANES - GLA_GATE_RANK)))
            w_gk2 = jnp.pad(gla_w_gk2[j], ((0, LANES - GLA_GATE_RANK), (0, 0)))
            mix_w = ((gla_w_in_b, j),
                     w_r.astype(BF16),
                     w_gk2.astype(BF16),
                     gla_b_gk2[j].reshape(1, GLA_KEY_DIM),
                     (gla_w_out_b, j),
                     gla_norm_g[j].reshape(1, GLA_DV),
                     ln_mix_g[i].reshape(1, D_MODEL),
                     ln_mix_b[i].reshape(1, D_MODEL))
            kind = "gla"
        ffn_w = ((ffn_w_up_b, i),
                 ffn_conv_w[i],
                 ffn_conv_b[i].reshape(1, D_FF),
                 (ffn_w_down_b, i),
                 ln_ffn_g[i].reshape(1, D_MODEL),
                 ln_ffn_b[i].reshape(1, D_MODEL))
        layer_weights.append((kind, mix_w, ffn_w))

    y_p, h_p, g_p, c_p = _trunk(x_prompt, None, None, None, layer_weights,
                                mixer_bb=1, ffn_bb=1, rows_t=PROMPT_ROWS, chunk=CHUNK)
    dec_b, dec_t = x_sample.shape[0], x_sample.shape[1]
    y_s, h_s, g_s, c_s = _trunk(x_sample, state_hgrn, state_gla, state_ffn_conv, layer_weights,
                                mixer_bb=8, ffn_bb=dec_b, rows_t=dec_t, chunk=dec_t)
    return (y_p, y_s, h_p, h_s, g_p, g_s, c_p, c_s)
```

```python
import functools
import math

import jax
import jax.numpy as jnp
from jax import lax
from jax.experimental import pallas as pl
from jax.experimental.pallas import tpu as pltpu

F32 = jnp.float32
BF16 = jnp.bfloat16

D_MODEL = 1024
DEPTH = 2
CHUNK = 64
HGRN_HEADS = 8
HGRN_DK = 128
HGRN_DV = 128
HGRN_FDIM = HGRN_HEADS * HGRN_DK
GLA_HEADS = 4
GLA_KEY_DIM = 512
GLA_VAL_DIM = 1024
GLA_DK = 128
GLA_DV = 256
GLA_GATE_RANK = 16
GLA_GATE_NORMALIZER = 16.0
D_FF = 2816
CONV_WIDTH = 3
ALPHA = (2.0 * DEPTH) ** 0.25
LN_EPS = 1e-5
RMS_EPS = 1e-6
LOG2E = 1.4426950408889634

SUBLANES = 8
LANES = 128
VMEM_LIMIT_BYTES = 56 * 1024 * 1024
PROMPT_ROWS = 512
FFN_COLS = 256
DOWN_COLS = 512
PROJ_COLS = 512
MAX_FACTOR_LOG2 = 100.0
FAST_BLOCK_ROWS = 2 * SUBLANES
MODE_SINGLE, MODE_BLOCKS, MODE_PAIRWISE = 2, 1, 0
TRIP_ROWS = 128


def _chunks_per_trip(chunk, n_chunks):
    return min(n_chunks, max(TRIP_ROWS // chunk, 1))


OUTPUT_OVERLAP_ITEMS = {"hgrn": 2, "gla": 3}


def _dot(a, b):
    return jnp.dot(a, b, preferred_element_type=F32)


def _dot_nt(a, b):
    return lax.dot_general(a, b, (((1,), (1,)), ((), ())), preferred_element_type=F32)


def _dot_tn(a, b):
    return lax.dot_general(a, b, (((0,), (0,)), ((), ())), preferred_element_type=F32)


def _sigmoid(x):
    return 1.0 / (1.0 + jnp.exp(-x))


def _silu(x):
    return x * _sigmoid(x)


def _gelu_tanh(x):
    c = math.sqrt(2.0 / math.pi)
    k = -2.0 * c * LOG2E
    z = x * ((k * 0.044715) * (x * x) + k)
    return x / (1.0 + jnp.exp2(z))


def _log2_sigmoid(x):
    return jnp.minimum(x, 0.0) * LOG2E - jnp.log2(1.0 + jnp.exp2(jnp.abs(x) * (-LOG2E)))


def _layer_norm(x, g, b):
    mu = jnp.mean(x, axis=-1, keepdims=True)
    xc = x - mu
    var = jnp.mean(xc * xc, axis=-1, keepdims=True)
    return xc * lax.rsqrt(var + LN_EPS) * g + b


def _cumsum_rows(g):
    c = g.shape[0]
    r = lax.broadcasted_iota(jnp.int32, (c, c), 0)
    s = lax.broadcasted_iota(jnp.int32, (c, c), 1)
    tri = (s <= r).astype(BF16)
    hi = g.astype(BF16)
    lo = (g - hi.astype(F32)).astype(BF16)
    return _dot(tri, hi) + _dot(tri, lo)


def _mode_limits(chunk):
    half_windows = max(chunk // 2 // SUBLANES, 1)
    block_windows = min(FAST_BLOCK_ROWS, chunk) // SUBLANES
    return MAX_FACTOR_LOG2 / half_windows, MAX_FACTOR_LOG2 / block_windows


def _head_attention(q_view, k_view, b_view, bc_ref, kc_ref, r0, h, dk, v, chunk, mode):
    exact_diag = mode == MODE_PAIRWISE
    if mode == MODE_PAIRWISE:
        R = SUBLANES
    elif mode == MODE_BLOCKS:
        R = min(FAST_BLOCK_ROWS, chunk)
    else:
        R = chunk
    ref_in_block = R // 2 - 1 if mode == MODE_SINGLE else R - 1
    nb = chunk // R
    static_rows = isinstance(r0, int)

    def cols(view):
        return slice(view[1] + h * dk, view[1] + (h + 1) * dk)

    def blk(view, i):
        if static_rows:
            return view[0][r0 + R * i:r0 + R * (i + 1), cols(view)]
        return view[0][pl.ds(pl.multiple_of(r0 + R * i, R), R), cols(view)]

    def row1(view, copy_ref, t):
        if static_rows:
            return view[0][r0 + t:r0 + t + 1, cols(view)]
        return copy_ref[t:t + 1, h * dk:(h + 1) * dk]

    def row(view, copy_ref, t):
        return jnp.broadcast_to(row1(view, copy_ref, t), (R, dk))

    def full(view):
        if static_rows:
            return view[0][r0:r0 + chunk, cols(view)]
        return view[0][pl.ds(r0, chunk), cols(view)]

    def b_end(j):
        return row(b_view, bc_ref, R * j + ref_in_block)

    b_last = row1(b_view, bc_ref, chunk - 1)
    kpp_f32 = [blk(k_view, j) * jnp.exp2(b_end(j) - blk(b_view, j)) for j in range(nb)]
    kpp = jnp.concatenate(kpp_f32, axis=0).astype(BF16)
    kd = jnp.concatenate(
        [kpp_f32[j] * jnp.exp2(b_last - row1(b_view, bc_ref, R * j + ref_in_block))
         for j in range(nb)], axis=0).astype(BF16)

    col = lax.broadcasted_iota(jnp.int32, (R, chunk), 1)
    rw = lax.broadcasted_iota(jnp.int32, (R, chunk), 0)
    colblk = col // R

    lq, pair = [], {}
    for i in range(nb):
        qi, bi = blk(q_view, i), blk(b_view, i)
        for j in range(i if exact_diag else i + 1):
            pair[(i, j)] = len(lq)
            lq.append(qi * jnp.exp2(bi - b_end(j)))
    if lq:
        cross = _dot_nt(jnp.concatenate(lq, axis=0).astype(BF16), kpp)

    def cross_blk(i, j):
        p = pair[(i, j)]
        return cross[R * p:R * (p + 1), :]

    vb = v.astype(BF16)
    qe = (full(q_view) * jnp.exp2(full(b_view))).astype(BF16)
    kv = _dot_tn(kd, vb)
    dv = v.shape[1]
    decay = jnp.broadcast_to(jnp.exp2(b_last), (dk, dk)).T
    if dv > dk:
        decay = jnp.concatenate([decay] * (dv // dk), axis=1)

    def finish(st):
        a_rows = []
        for i in range(nb):
            if exact_diag:
                acc = jnp.zeros((R, chunk), F32)
                for j in range(i):
                    acc = jnp.where(colblk == j, cross_blk(i, j), acc)
                qi, bi = blk(q_view, i), blk(b_view, i)
                for s in range(R):
                    t = R * i + s
                    e = jnp.exp2(jnp.minimum(bi - row(b_view, bc_ref, t), 0.0))
                    cs = jnp.sum(qi * e * row(k_view, kc_ref, t), axis=-1, keepdims=True)
                    acc = jnp.where(col == t, cs, acc)
            else:
                acc = cross_blk(i, 0)
                for j in range(1, i + 1):
                    acc = jnp.where(colblk == j, cross_blk(i, j), acc)
            a_rows.append(jnp.where(col <= R * i + rw, acc, 0.0))
        a = jnp.concatenate(a_rows, axis=0).astype(BF16)
        o = _dot(jnp.concatenate([qe, a], axis=1),
                 jnp.concatenate([st.astype(BF16), vb], axis=0))
        st_new = decay * st + kv
        return o, st_new

    return finish


def _rms_gate(o, norm_g, gate_act):
    ms = jnp.mean(o * o, axis=-1, keepdims=True)
    return o * lax.rsqrt(ms + RMS_EPS) * norm_g * gate_act


def _mixer_kernel(*refs, kind, layer, bb, nch, chunk, nt, has_s0):
    it = iter(refs)
    x_ref = next(it)
    xp_ref = next(it)
    s0_ref = next(it) if has_s0 else None
    if kind == "hgrn":
        lb_ref = next(it)
        w_in_ref = next(it)
    else:
        w_in_ref = next(it)
        w_r_ref = next(it)
        w_gk2_ref = next(it)
        b_gk2_ref = next(it)
    w_out_ref = next(it)
    ng_ref = next(it)
    lng_ref = next(it)
    lnb_ref = next(it)
    y_ref = next(it)
    so_ref = next(it)
    proj2_ref = next(it)
    aux2_ref = next(it)
    bc_ref = next(it)
    kc_ref = next(it)
    o_ref = next(it)
    st_ref = next(it)
    flag_ref = next(it)

    heads, dk, dv = (HGRN_HEADS, HGRN_DK, HGRN_DV) if kind == "hgrn" else (GLA_HEADS, GLA_DK, GLA_DV)
    kdim = heads * dk
    rows = bb * nch * chunk
    n_chunks = bb * nch
    per_item = _chunks_per_trip(chunk, n_chunks)
    nb = chunk // SUBLANES
    ncols = proj2_ref.shape[2]
    s = pl.program_id(1)
    ns = pl.num_programs(1)
    cur = s % 2
    prv = 1 - cur
    has_cur = s < ns - 1
    has_prev = s > 0
    prev_t = (s - 1) % nt

    if kind == "hgrn":
        q_off, f_off, v_off, g_off = 0, kdim, 2 * kdim, 3 * kdim
    else:
        q_off, k_off, v_off, g_off = 0, kdim, 2 * kdim, 2 * kdim + heads * dv

    def views(slot):
        proj, aux = proj2_ref.at[slot], aux2_ref.at[slot]
        if kind == "hgrn":
            return proj, aux, (proj, q_off), (aux, 0), (proj, f_off)
        return proj, aux, (proj, q_off), (proj, k_off), (aux, 0)

    @pl.when(s == 0)
    def _init_flags():
        flag_ref[0] = 0
        flag_ref[1] = 0

    @pl.when(has_prev & (prev_t == 0))
    def _init_state():
        if has_s0:
            st_ref[...] = s0_ref[...]
        else:
            st_ref[...] = jnp.zeros(st_ref.shape, F32)

    ng = ng_ref[...]

    def projection_items(slot):
        proj, aux, _, _, _ = views(slot)
        xb = x_ref[...].reshape(rows, D_MODEL).astype(BF16)
        decay_max = []

        def cumsum_in_place(ref, cs):
            m = jnp.zeros((SUBLANES, cs.stop - cs.start), F32)
            for c in range(n_chunks):
                rs = slice(c * chunk, (c + 1) * chunk)
                b = _cumsum_rows(ref[rs, cs])
                ref[rs, cs] = b
                m = jnp.maximum(m, -b[0:SUBLANES])
                for j in range(1, nb):
                    m = jnp.maximum(m, b[SUBLANES * (j - 1):SUBLANES * j] - b[SUBLANES * j:SUBLANES * (j + 1)])
            decay_max.append(jnp.max(m))

        if kind == "hgrn":
            p = lb_ref[...]
            pe = jnp.exp(p - jnp.max(p, axis=0, keepdims=True))
            lb = jnp.sum(pe[:layer + 1], axis=0, keepdims=True) / jnp.sum(pe, axis=0, keepdims=True)

        def col_block(n0):
            cs = slice(n0, n0 + PROJ_COLS)
            blk = _dot(xb, w_in_ref[:, cs])
            if n0 >= g_off:
                proj[:, cs] = _silu(blk)
            elif n0 >= v_off:
                proj[:, cs] = blk
            elif kind == "hgrn" and n0 >= f_off:
                fs = slice(n0 - f_off, n0 - f_off + PROJ_COLS)
                f = lb[:, fs] + (1.0 - lb[:, fs]) * _sigmoid(blk)
                aux[:, fs] = 1.0 - f
                proj[:, cs] = jnp.log2(f)
                cumsum_in_place(proj, cs)
            elif kind == "hgrn":
                proj[:, cs] = _silu(blk)
            elif n0 >= k_off:
                proj[:, cs] = blk
            else:
                proj[:, cs] = blk * (dk ** -0.5)

        def gla_gate():
            r = _dot(xb, w_r_ref[...]).astype(BF16)
            z = _dot(r, w_gk2_ref[...]) + b_gk2_ref[...]
            aux[...] = _log2_sigmoid(z) * (1.0 / GLA_GATE_NORMALIZER)
            cumsum_in_place(aux, slice(0, kdim))

        def record_bound():
            worst = functools.reduce(jnp.maximum, decay_max)
            single_max, blocks_max = _mode_limits(chunk)
            flag_ref[slot] = jnp.where(
                worst <= single_max, MODE_SINGLE,
                jnp.where(worst <= blocks_max, MODE_BLOCKS, MODE_PAIRWISE)).astype(jnp.int32)

        items = [functools.partial(col_block, n0) for n0 in range(0, ncols, PROJ_COLS)]
        if kind == "gla":
            items.append(gla_gate)
        return items, record_bound

    def finish_chunk(proj, c_rows, b_idx, finishers):
        for h, finish in enumerate(finishers):
            o, st_new = finish(st_ref[b_idx, h])
            st_ref[b_idx, h] = st_new
            gate = proj[c_rows, g_off + h * dv:g_off + (h + 1) * dv]
            o_ref[c_rows, h * dv:(h + 1) * dv] = _rms_gate(o, ng, gate).astype(BF16)

    def attention_items(slot):
        proj, _, q_view, k_view, b_view = views(slot)
        items, staged = [], {}

        def issue(chunks):
            for c in chunks:
                r0 = c * chunk
                staged[c] = [_head_attention(q_view, k_view, b_view, None, None, r0, h, dk,
                                             proj[r0:r0 + chunk, v_off + h * dv:v_off + (h + 1) * dv],
                                             chunk, MODE_SINGLE) for h in range(heads)]

        def consume(chunks):
            for c in chunks:
                finish_chunk(proj, slice(c * chunk, (c + 1) * chunk), c // nch, staged.pop(c))

        for c0 in range(0, n_chunks, per_item):
            chunks = list(range(c0, c0 + per_item))
            items += [functools.partial(issue, chunks), functools.partial(consume, chunks)]
        return items

    def attention_loop(slot, mode):
        exact_diag = mode == MODE_PAIRWISE
        proj, _, q_view, k_view, b_view = views(slot)

        def first_stage(ci, copy):
            r0 = pl.multiple_of(ci * chunk, chunk)
            rs = pl.ds(r0, chunk)
            bc, kc = bc_ref.at[copy], kc_ref.at[copy]
            bc[...] = b_view[0][rs, b_view[1]:b_view[1] + kdim]
            if exact_diag:
                kc[...] = k_view[0][rs, k_view[1]:k_view[1] + kdim]
            return [_head_attention(q_view, k_view, b_view, bc, kc, r0, h, dk,
                                    proj[rs, v_off + h * dv:v_off + (h + 1) * dv],
                                    chunk, mode) for h in range(heads)]

        per_trip = 1 if exact_diag else per_item

        def trip(ti, carry):
            staged = [first_stage(ti * per_trip + j, j) for j in range(per_trip)]
            for j in range(per_trip):
                ci = ti * per_trip + j
                finish_chunk(proj, pl.ds(pl.multiple_of(ci * chunk, chunk), chunk), ci // nch, staged[j])
            return carry
        lax.fori_loop(0, n_chunks // per_trip, trip, 0)

    prev_mode = flag_ref[prv]
    overlap = has_cur & has_prev & (prev_mode == MODE_SINGLE)

    def output_matmul():
        return _dot(o_ref[...], w_out_ref[...])

    def output_norm(y):
        xp = xp_ref[...].reshape(rows, D_MODEL)
        out = _layer_norm(ALPHA * xp + y, lng_ref[...], lnb_ref[...])
        y_ref[...] = out.reshape(y_ref.shape)

    def project_and_attend(slot):
        proj_items, record_bound = projection_items(slot)
        attn_items = attention_items(1 - slot)
        held_back = proj_items[len(proj_items) - OUTPUT_OVERLAP_ITEMS[kind]:]
        proj_items = proj_items[:len(proj_items) - OUTPUT_OVERLAP_ITEMS[kind]]
        done = 0
        for i, attend in enumerate(attn_items):
            upto = -(-len(proj_items) * (i + 1) // len(attn_items))
            for item in proj_items[done:upto]:
                item()
            done = upto
            attend()
        y = output_matmul()
        for item in held_back[:-1]:
            item()
        output_norm(y)
        held_back[-1]()
        record_bound()

    for slot in range(2):
        pl.when(overlap & (cur == slot))(functools.partial(project_and_attend, slot))

    @pl.when(jnp.logical_not(overlap))
    def _one_at_a_time():
        for mode in (MODE_SINGLE, MODE_BLOCKS, MODE_PAIRWISE):
            pl.when(has_prev & (prev_mode == mode))(functools.partial(attention_loop, prv, mode))

        @pl.when(has_cur)
        def _project():
            proj_items, record_bound = projection_items(cur)
            for item in proj_items:
                item()
            record_bound()

    @pl.when(has_prev & jnp.logical_not(overlap))
    def _output():
        output_norm(output_matmul())

    @pl.when(has_prev & (prev_t == nt - 1))
    def _write_state():
        so_ref[...] = st_ref[...]


def _weight_operand(w):
    if isinstance(w, tuple):
        arr, layer = w
        nd = arr.ndim - 1
        return arr, pl.BlockSpec((None,) + arr.shape[1:], lambda b, s: (layer,) + (0,) * nd,
                                 pipeline_mode=pl.Buffered(1))
    nd = w.ndim
    return w, pl.BlockSpec(w.shape, lambda b, s: (0,) * nd, pipeline_mode=pl.Buffered(1))


def _mixer_call(kind, layer, x, s0, weights, *, bb, rows_t, chunk):
    B, T, D = x.shape
    heads, dk, dv = (HGRN_HEADS, HGRN_DK, HGRN_DV) if kind == "hgrn" else (GLA_HEADS, GLA_DK, GLA_DV)
    nch = rows_t // chunk
    rows = bb * rows_t
    nt = T // rows_t
    n_blocks = (B // bb) * nt
    grid = (1, n_blocks + 1)
    has_s0 = s0 is not None

    def this_block(_, s):
        blk = jnp.minimum(s, n_blocks - 1)
        return (blk // nt, blk % nt, 0)

    def prev_block(_, s):
        blk = jnp.maximum(s - 1, 0)
        return (blk // nt, blk % nt, 0)

    def prev_group(_, s):
        return (jnp.maximum(s - 1, 0) // nt, 0, 0, 0)

    in_specs = [pl.BlockSpec((bb, rows_t, D), this_block),
                pl.BlockSpec((bb, rows_t, D), prev_block)]
    args = [x, x]
    if has_s0:
        in_specs.append(pl.BlockSpec((bb, heads, dk, dv), prev_group))
        args.append(s0)
    for w in weights:
        arr, spec = _weight_operand(w)
        in_specs.append(spec)
        args.append(arr)

    ncols = 3 * HGRN_FDIM + D_MODEL if kind == "hgrn" else 2 * GLA_KEY_DIM + 2 * GLA_VAL_DIM
    scratch = [
        pltpu.VMEM((2, rows, ncols), F32),
        pltpu.VMEM((2, rows, heads * dk), F32),
        pltpu.VMEM((_chunks_per_trip(chunk, bb * nch), chunk, heads * dk), F32),
        pltpu.VMEM((_chunks_per_trip(chunk, bb * nch), chunk, heads * dk), F32),
        pltpu.VMEM((rows, heads * dv), BF16),
        pltpu.VMEM((bb, heads, dk, dv), F32),
        pltpu.SMEM((2,), jnp.int32),
    ]
    kern = functools.partial(_mixer_kernel, kind=kind, layer=layer, bb=bb, nch=nch, chunk=chunk,
                             nt=nt, has_s0=has_s0)
    return pl.pallas_call(
        kern,
        grid=grid,
        in_specs=in_specs,
        out_specs=[pl.BlockSpec((bb, rows_t, D), prev_block),
                   pl.BlockSpec((bb, heads, dk, dv), prev_group)],
        out_shape=[jax.ShapeDtypeStruct((B, T, D), F32),
                   jax.ShapeDtypeStruct((B, heads, dk, dv), F32)],
        scratch_shapes=scratch,
        compiler_params=pltpu.CompilerParams(
            dimension_semantics=("arbitrary", "arbitrary"),
            vmem_limit_bytes=VMEM_LIMIT_BYTES),
        name=f"{kind}_mixer_{'s' if has_s0 else 'p'}",
    )(*args)


def _ffn_kernel(*refs, bb, rows_t, has_s0):
    it = iter(refs)
    x_ref = next(it)
    c0_ref = next(it) if has_s0 else None
    w_up_ref = next(it)
    cw_ref = next(it)
    cb_ref = next(it)
    w_dn_ref = next(it)
    lng_ref = next(it)
    lnb_ref = next(it)
    y_ref = next(it)
    co_ref = next(it)
    a_ref = next(it)
    h_ref = next(it)
    acc_ref = next(it)

    rows = bb * rows_t
    t = pl.program_id(1)
    nt = pl.num_programs(1)
    pad = SUBLANES

    @pl.when(t == 0)
    def _init_carry():
        if has_s0:
            a_ref[:, pad - 2:pad, :] = c0_ref[...]
        else:
            a_ref[:, pad - 2:pad, :] = jnp.zeros((bb, 2, D_FF), F32)

    x = x_ref[...].reshape(rows, D_MODEL)
    xb = x.astype(BF16)

    def up_proj(n0):
        cs = slice(n0, n0 + FFN_COLS)
        a = _dot(xb, w_up_ref[:, cs])
        a_ref[:, pad:pad + rows_t, cs] = a.reshape(bb, rows_t, FFN_COLS)
        return _dot(xb, w_up_ref[:, D_FF + n0:D_FF + n0 + FFN_COLS])

    def activate(n0, g):
        cs = slice(n0, n0 + FFN_COLS)
        conv = (cb_ref[:, cs]
                + a_ref[:, pad - 2:pad - 2 + rows_t, cs] * cw_ref[0:1, cs]
                + a_ref[:, pad - 1:pad - 1 + rows_t, cs] * cw_ref[1:2, cs]
                + a_ref[:, pad:pad + rows_t, cs] * cw_ref[2:3, cs])
        h_ref[:, cs] = (_gelu_tanh(conv).reshape(rows, FFN_COLS) * g).astype(BF16)

    starts = list(range(0, D_FF, FFN_COLS))
    g_next = up_proj(starts[0])
    for idx, n0 in enumerate(starts):
        g = g_next
        if idx + 1 < len(starts):
            g_next = up_proj(starts[idx + 1])
        activate(n0, g)

    for n0 in range(0, D_MODEL, DOWN_COLS):
        acc_ref[:, n0:n0 + DOWN_COLS] = _dot(h_ref[...], w_dn_ref[:, n0:n0 + DOWN_COLS])

    out = _layer_norm(ALPHA * x + acc_ref[...], lng_ref[...], lnb_ref[...])
    y_ref[...] = out.reshape(y_ref.shape)

    last2 = a_ref[:, pad + rows_t - 2:pad + rows_t, :]
    a_ref[:, pad - 2:pad, :] = last2

    @pl.when(t == nt - 1)
    def _write_carry():
        co_ref[...] = last2


def _ffn_call(x, c0, weights, *, bb, rows_t):
    B, T, D = x.shape
    grid = (B // bb, T // rows_t)
    has_s0 = c0 is not None
    in_specs = [pl.BlockSpec((bb, rows_t, D), lambda b, t: (b, t, 0))]
    args = [x]
    if has_s0:
        in_specs.append(pl.BlockSpec((bb, CONV_WIDTH - 1, D_FF), lambda b, t: (b, 0, 0)))
        args.append(c0)
    for w in weights:
        arr, spec = _weight_operand(w)
        in_specs.append(spec)
        args.append(arr)
    kern = functools.partial(_ffn_kernel, bb=bb, rows_t=rows_t, has_s0=has_s0)
    return pl.pallas_call(
        kern,
        grid=grid,
        in_specs=in_specs,
        out_specs=[pl.BlockSpec((bb, rows_t, D), lambda b, t: (b, t, 0)),
                   pl.BlockSpec((bb, CONV_WIDTH - 1, D_FF), lambda b, t: (b, 0, 0))],
        out_shape=[jax.ShapeDtypeStruct((B, T, D), F32),
                   jax.ShapeDtypeStruct((B, CONV_WIDTH - 1, D_FF), F32)],
        scratch_shapes=[pltpu.VMEM((bb, SUBLANES + rows_t, D_FF), F32),
                        pltpu.VMEM((bb * rows_t, D_FF), BF16),
                        pltpu.VMEM((bb * rows_t, D), F32)],
        compiler_params=pltpu.CompilerParams(
            dimension_semantics=("arbitrary", "arbitrary"),
            vmem_limit_bytes=VMEM_LIMIT_BYTES),
        name=f"conv_ffn_{'s' if has_s0 else 'p'}",
    )(*args)


def _trunk(x, st_hgrn, st_gla, st_conv, layer_weights, *, mixer_bb, ffn_bb, rows_t, chunk):
    new_h, new_g, new_c = [], [], []
    for i in range(DEPTH):
        kind, mix_w, ffn_w = layer_weights[i]
        j = i // 2
        if kind == "hgrn":
            s0 = None if st_hgrn is None else st_hgrn[j]
        else:
            s0 = None if st_gla is None else st_gla[j]
        x, s = _mixer_call(kind, i, x, s0, mix_w, bb=mixer_bb, rows_t=rows_t, chunk=chunk)
        (new_h if kind == "hgrn" else new_g).append(s)
        c0 = None if st_conv is None else st_conv[i]
        x, c = _ffn_call(x, c0, ffn_w, bb=ffn_bb, rows_t=rows_t)
        new_c.append(c)
    return x, jnp.stack(new_h), jnp.stack(new_g), jnp.stack(new_c)


def kernel(x_prompt, x_sample, state_hgrn, state_gla, state_ffn_conv, lb_param, hgrn_w_in, hgrn_w_out, hgrn_norm_g, gla_w_in, gla_w_gk2, gla_b_gk2, gla_w_out, gla_norm_g, ln_mix_g, ln_mix_b, ffn_w_up, ffn_conv_w, ffn_conv_b, ffn_w_down, ln_ffn_g, ln_ffn_b):
    gla_main = 2 * GLA_KEY_DIM + 2 * GLA_VAL_DIM
    hgrn_w_in_b, hgrn_w_out_b = hgrn_w_in.astype(BF16), hgrn_w_out.astype(BF16)
    gla_w_in_b, gla_w_out_b = gla_w_in.astype(BF16), gla_w_out.astype(BF16)
    ffn_w_up_b, ffn_w_down_b = ffn_w_up.astype(BF16), ffn_w_down.astype(BF16)
    layer_weights = []
    for i in range(DEPTH):
        j = i // 2
        if i % 2 == 0:
            mix_w = (lb_param.astype(F32),
                     (hgrn_w_in_b, j),
                     (hgrn_w_out_b, j),
                     hgrn_norm_g[j].reshape(1, HGRN_DV),
                     ln_mix_g[i].reshape(1, D_MODEL),
                     ln_mix_b[i].reshape(1, D_MODEL))
            kind = "hgrn"
        else:
            w_r = jnp.pad(gla_w_in[j][:, gla_main:], ((0, 0), (0, LANES - GLA_GATE_RANK)))
            w_gk2 = jnp.pad(gla_w_gk2[j], ((0, LANES - GLA_GATE_RANK), (0, 0)))
            mix_w = ((gla_w_in_b, j),
                     w_r.astype(BF16),
                     w_gk2.astype(BF16),
                     gla_b_gk2[j].reshape(1, GLA_KEY_DIM),
                     (gla_w_out_b, j),
                     gla_norm_g[j].reshape(1, GLA_DV),
                     ln_mix_g[i].reshape(1, D_MODEL),
                     ln_mix_b[i].reshape(1, D_MODEL))
            kind = "gla"
        ffn_w = ((ffn_w_up_b, i),
                 ffn_conv_w[i],
                 ffn_conv_b[i].reshape(1, D_FF),
                 (ffn_w_down_b, i),
                 ln_ffn_g[i].reshape(1, D_MODEL),
                 ln_ffn_b[i].reshape(1, D_MODEL))
        layer_weights.append((kind, mix_w, ffn_w))

    y_p, h_p, g_p, c_p = _trunk(x_prompt, None, None, None, layer_weights,
                                mixer_bb=1, ffn_bb=1, rows_t=PROMPT_ROWS, chunk=CHUNK)
    dec_b, dec_t = x_sample.shape[0], x_sample.shape[1]
    y_s, h_s, g_s, c_s = _trunk(x_sample, state_hgrn, state_gla, state_ffn_conv, layer_weights,
                                mixer_bb=8, ffn_bb=dec_b, rows_t=dec_t, chunk=dec_t)
    return (y_p, y_s, h_p, h_s, g_p, g_s, c_p, c_s)
```

```python
import functools
import math

import jax
import jax.numpy as jnp
from jax import lax
from jax.experimental import pallas as pl
from jax.experimental.pallas import tpu as pltpu

F32 = jnp.float32
BF16 = jnp.bfloat16

D_MODEL = 1024
DEPTH = 2
CHUNK = 64
HGRN_HEADS = 8
HGRN_DK = 128
HGRN_DV = 128
HGRN_FDIM = HGRN_HEADS * HGRN_DK
GLA_HEADS = 4
GLA_KEY_DIM = 512
GLA_VAL_DIM = 1024
GLA_DK = 128
GLA_DV = 256
GLA_GATE_RANK = 16
GLA_GATE_NORMALIZER = 16.0
D_FF = 2816
CONV_WIDTH = 3
ALPHA = (2.0 * DEPTH) ** 0.25
LN_EPS = 1e-5
RMS_EPS = 1e-6
LOG2E = 1.4426950408889634

SUBLANES = 8
LANES = 128
VMEM_LIMIT_BYTES = 56 * 1024 * 1024
PROMPT_ROWS = 512
FFN_COLS = 256
DOWN_COLS = 512
PROJ_COLS = 512
MAX_FACTOR_LOG2 = 100.0
FAST_BLOCK_ROWS = 2 * SUBLANES
MODE_SINGLE, MODE_BLOCKS, MODE_PAIRWISE = 2, 1, 0
TRIP_ROWS = 128


def _chunks_per_trip(chunk, n_chunks):
    return min(n_chunks, max(TRIP_ROWS // chunk, 1))


OUTPUT_OVERLAP_ITEMS = {"hgrn": 2, "gla": 4}


def _dot(a, b):
    return jnp.dot(a, b, preferred_element_type=F32)


def _dot_nt(a, b):
    return lax.dot_general(a, b, (((1,), (1,)), ((), ())), preferred_element_type=F32)


def _dot_tn(a, b):
    return lax.dot_general(a, b, (((0,), (0,)), ((), ())), preferred_element_type=F32)


def _sigmoid(x):
    return 1.0 / (1.0 + jnp.exp(-x))


def _silu(x):
    return x * _sigmoid(x)


def _gelu_tanh(x):
    c = math.sqrt(2.0 / math.pi)
    k = -2.0 * c * LOG2E
    z = x * ((k * 0.044715) * (x * x) + k)
    return x / (1.0 + jnp.exp2(z))


def _log2_sigmoid(x):
    return jnp.minimum(x, 0.0) * LOG2E - jnp.log2(1.0 + jnp.exp2(jnp.abs(x) * (-LOG2E)))


def _layer_norm(x, g, b):
    mu = jnp.mean(x, axis=-1, keepdims=True)
    xc = x - mu
    var = jnp.mean(xc * xc, axis=-1, keepdims=True)
    return xc * lax.rsqrt(var + LN_EPS) * g + b


def _cumsum_rows(g):
    c = g.shape[0]
    r = lax.broadcasted_iota(jnp.int32, (c, c), 0)
    s = lax.broadcasted_iota(jnp.int32, (c, c), 1)
    tri = (s <= r).astype(BF16)
    hi = g.astype(BF16)
    lo = (g - hi.astype(F32)).astype(BF16)
    return _dot(tri, hi) + _dot(tri, lo)


def _mode_limits(chunk):
    half_windows = max(chunk // 2 // SUBLANES, 1)
    block_windows = min(FAST_BLOCK_ROWS, chunk) // SUBLANES
    return MAX_FACTOR_LOG2 / half_windows, MAX_FACTOR_LOG2 / block_windows


def _head_attention(q_view, k_view, b_view, bc_ref, kc_ref, r0, h, dk, v, chunk, mode):
    exact_diag = mode == MODE_PAIRWISE
    if mode == MODE_PAIRWISE:
        R = SUBLANES
    elif mode == MODE_BLOCKS:
        R = min(FAST_BLOCK_ROWS, chunk)
    else:
        R = chunk
    ref_in_block = R // 2 - 1 if mode == MODE_SINGLE else R - 1
    nb = chunk // R
    static_rows = isinstance(r0, int)

    def cols(view):
        return slice(view[1] + h * dk, view[1] + (h + 1) * dk)

    def blk(view, i):
        if static_rows:
            return view[0][r0 + R * i:r0 + R * (i + 1), cols(view)]
        return view[0][pl.ds(pl.multiple_of(r0 + R * i, R), R), cols(view)]

    def row1(view, copy_ref, t):
        if static_rows:
            return view[0][r0 + t:r0 + t + 1, cols(view)]
        return copy_ref[t:t + 1, h * dk:(h + 1) * dk]

    def row(view, copy_ref, t):
        return jnp.broadcast_to(row1(view, copy_ref, t), (R, dk))

    def full(view):
        if static_rows:
            return view[0][r0:r0 + chunk, cols(view)]
        return view[0][pl.ds(r0, chunk), cols(view)]

    def b_end(j):
        return row(b_view, bc_ref, R * j + ref_in_block)

    b_last = row1(b_view, bc_ref, chunk - 1)
    kpp_f32 = [blk(k_view, j) * jnp.exp2(b_end(j) - blk(b_view, j)) for j in range(nb)]
    kpp = jnp.concatenate(kpp_f32, axis=0).astype(BF16)
    kd = jnp.concatenate(
        [kpp_f32[j] * jnp.exp2(b_last - row1(b_view, bc_ref, R * j + ref_in_block))
         for j in range(nb)], axis=0).astype(BF16)

    col = lax.broadcasted_iota(jnp.int32, (R, chunk), 1)
    rw = lax.broadcasted_iota(jnp.int32, (R, chunk), 0)
    colblk = col // R

    lq, pair = [], {}
    for i in range(nb):
        qi, bi = blk(q_view, i), blk(b_view, i)
        for j in range(i if exact_diag else i + 1):
            pair[(i, j)] = len(lq)
            lq.append(qi * jnp.exp2(bi - b_end(j)))
    if lq:
        cross = _dot_nt(jnp.concatenate(lq, axis=0).astype(BF16), kpp)

    def cross_blk(i, j):
        p = pair[(i, j)]
        return cross[R * p:R * (p + 1), :]

    vb = v.astype(BF16)
    qe = (full(q_view) * jnp.exp2(full(b_view))).astype(BF16)
    kv = _dot_tn(kd, vb)
    dv = v.shape[1]
    decay = jnp.broadcast_to(jnp.exp2(b_last), (dk, dk)).T
    if dv > dk:
        decay = jnp.concatenate([decay] * (dv // dk), axis=1)

    def finish(st):
        a_rows = []
        for i in range(nb):
            if exact_diag:
                acc = jnp.zeros((R, chunk), F32)
                for j in range(i):
                    acc = jnp.where(colblk == j, cross_blk(i, j), acc)
                qi, bi = blk(q_view, i), blk(b_view, i)
                for s in range(R):
                    t = R * i + s
                    e = jnp.exp2(jnp.minimum(bi - row(b_view, bc_ref, t), 0.0))
                    cs = jnp.sum(qi * e * row(k_view, kc_ref, t), axis=-1, keepdims=True)
                    acc = jnp.where(col == t, cs, acc)
            else:
                acc = cross_blk(i, 0)
                for j in range(1, i + 1):
                    acc = jnp.where(colblk == j, cross_blk(i, j), acc)
            a_rows.append(jnp.where(col <= R * i + rw, acc, 0.0))
        a = jnp.concatenate(a_rows, axis=0).astype(BF16)
        o = _dot(jnp.concatenate([qe, a], axis=1),
                 jnp.concatenate([st.astype(BF16), vb], axis=0))
        st_new = decay * st + kv
        return o, st_new

    return finish


def _rms_gate(o, norm_g, gate_act):
    ms = jnp.mean(o * o, axis=-1, keepdims=True)
    return o * lax.rsqrt(ms + RMS_EPS) * norm_g * gate_act


def _mixer_kernel(*refs, kind, layer, bb, nch, chunk, nt, has_s0):
    it = iter(refs)
    x_ref = next(it)
    xp_ref = next(it)
    s0_ref = next(it) if has_s0 else None
    if kind == "hgrn":
        lb_ref = next(it)
        w_in_ref = next(it)
    else:
        w_in_ref = next(it)
        w_r_ref = next(it)
        w_gk2_ref = next(it)
        b_gk2_ref = next(it)
    w_out_ref = next(it)
    ng_ref = next(it)
    lng_ref = next(it)
    lnb_ref = next(it)
    y_ref = next(it)
    so_ref = next(it)
    proj2_ref = next(it)
    aux2_ref = next(it)
    bc_ref = next(it)
    kc_ref = next(it)
    o_ref = next(it)
    st_ref = next(it)
    flag_ref = next(it)

    heads, dk, dv = (HGRN_HEADS, HGRN_DK, HGRN_DV) if kind == "hgrn" else (GLA_HEADS, GLA_DK, GLA_DV)
    kdim = heads * dk
    rows = bb * nch * chunk
    n_chunks = bb * nch
    per_item = _chunks_per_trip(chunk, n_chunks)
    nb = chunk // SUBLANES
    ncols = proj2_ref.shape[2]
    s = pl.program_id(1)
    ns = pl.num_programs(1)
    cur = s % 2
    prv = 1 - cur
    has_cur = s < ns - 1
    has_prev = s > 0
    prev_t = (s - 1) % nt

    if kind == "hgrn":
        q_off, f_off, v_off, g_off = 0, kdim, 2 * kdim, 3 * kdim
    else:
        q_off, k_off, v_off, g_off = 0, kdim, 2 * kdim, 2 * kdim + heads * dv

    def views(slot):
        proj, aux = proj2_ref.at[slot], aux2_ref.at[slot]
        if kind == "hgrn":
            return proj, aux, (proj, q_off), (aux, 0), (proj, f_off)
        return proj, aux, (proj, q_off), (proj, k_off), (aux, 0)

    @pl.when(s == 0)
    def _init_flags():
        flag_ref[0] = 0
        flag_ref[1] = 0

    @pl.when(has_prev & (prev_t == 0))
    def _init_state():
        if has_s0:
            st_ref[...] = s0_ref[...]
        else:
            st_ref[...] = jnp.zeros(st_ref.shape, F32)

    ng = ng_ref[...]

    def projection_items(slot):
        proj, aux, _, _, _ = views(slot)
        xb = x_ref[...].reshape(rows, D_MODEL).astype(BF16)
        decay_max = []

        def cumsum_in_place(ref, cs):
            m = jnp.zeros((SUBLANES, cs.stop - cs.start), F32)
            for c in range(n_chunks):
                rs = slice(c * chunk, (c + 1) * chunk)
                b = _cumsum_rows(ref[rs, cs])
                ref[rs, cs] = b
                m = jnp.maximum(m, -b[0:SUBLANES])
                for j in range(1, nb):
                    m = jnp.maximum(m, b[SUBLANES * (j - 1):SUBLANES * j] - b[SUBLANES * j:SUBLANES * (j + 1)])
            decay_max.append(jnp.max(m))

        if kind == "hgrn":
            p = lb_ref[...]
            pe = jnp.exp(p - jnp.max(p, axis=0, keepdims=True))
            lb = jnp.sum(pe[:layer + 1], axis=0, keepdims=True) / jnp.sum(pe, axis=0, keepdims=True)

        def col_block(n0):
            cs = slice(n0, n0 + PROJ_COLS)
            blk = _dot(xb, w_in_ref[:, cs])
            if n0 >= g_off:
                proj[:, cs] = _silu(blk)
            elif n0 >= v_off:
                proj[:, cs] = blk
            elif kind == "hgrn" and n0 >= f_off:
                fs = slice(n0 - f_off, n0 - f_off + PROJ_COLS)
                f = lb[:, fs] + (1.0 - lb[:, fs]) * _sigmoid(blk)
                aux[:, fs] = 1.0 - f
                proj[:, cs] = jnp.log2(f)
                cumsum_in_place(proj, cs)
            elif kind == "hgrn":
                proj[:, cs] = _silu(blk)
            elif n0 >= k_off:
                proj[:, cs] = blk
            else:
                proj[:, cs] = blk * (dk ** -0.5)

        def gla_gate():
            r = _dot(xb, w_r_ref[...]).astype(BF16)
            z = _dot(r, w_gk2_ref[...]) + b_gk2_ref[...]
            aux[...] = _log2_sigmoid(z) * (1.0 / GLA_GATE_NORMALIZER)
            cumsum_in_place(aux, slice(0, kdim))

        def record_bound():
            worst = functools.reduce(jnp.maximum, decay_max)
            single_max, blocks_max = _mode_limits(chunk)
            flag_ref[slot] = jnp.where(
                worst <= single_max, MODE_SINGLE,
                jnp.where(worst <= blocks_max, MODE_BLOCKS, MODE_PAIRWISE)).astype(jnp.int32)

        items = [functools.partial(col_block, n0) for n0 in range(0, ncols, PROJ_COLS)]
        if kind == "gla":
            items.append(gla_gate)
        return items, record_bound

    def finish_chunk(proj, c_rows, b_idx, finishers):
        for h, finish in enumerate(finishers):
            o, st_new = finish(st_ref[b_idx, h])
            st_ref[b_idx, h] = st_new
            gate = proj[c_rows, g_off + h * dv:g_off + (h + 1) * dv]
            o_ref[c_rows, h * dv:(h + 1) * dv] = _rms_gate(o, ng, gate).astype(BF16)

    def attention_items(slot):
        proj, _, q_view, k_view, b_view = views(slot)
        items, staged = [], {}

        def issue(chunks):
            for c in chunks:
                r0 = c * chunk
                staged[c] = [_head_attention(q_view, k_view, b_view, None, None, r0, h, dk,
                                             proj[r0:r0 + chunk, v_off + h * dv:v_off + (h + 1) * dv],
                                             chunk, MODE_SINGLE) for h in range(heads)]

        def consume(chunks):
            for c in chunks:
                finish_chunk(proj, slice(c * chunk, (c + 1) * chunk), c // nch, staged.pop(c))

        for c0 in range(0, n_chunks, per_item):
            chunks = list(range(c0, c0 + per_item))
            items += [functools.partial(issue, chunks), functools.partial(consume, chunks)]
        return items

    def attention_loop(slot, mode):
        exact_diag = mode == MODE_PAIRWISE
        proj, _, q_view, k_view, b_view = views(slot)

        def first_stage(ci, copy):
            r0 = pl.multiple_of(ci * chunk, chunk)
            rs = pl.ds(r0, chunk)
            bc, kc = bc_ref.at[copy], kc_ref.at[copy]
            bc[...] = b_view[0][rs, b_view[1]:b_view[1] + kdim]
            if exact_diag:
                kc[...] = k_view[0][rs, k_view[1]:k_view[1] + kdim]
            return [_head_attention(q_view, k_view, b_view, bc, kc, r0, h, dk,
                                    proj[rs, v_off + h * dv:v_off + (h + 1) * dv],
                                    chunk, mode) for h in range(heads)]

        per_trip = 1 if exact_diag else per_item

        def trip(ti, carry):
            staged = [first_stage(ti * per_trip + j, j) for j in range(per_trip)]
            for j in range(per_trip):
                ci = ti * per_trip + j
                finish_chunk(proj, pl.ds(pl.multiple_of(ci * chunk, chunk), chunk), ci // nch, staged[j])
            return carry
        lax.fori_loop(0, n_chunks // per_trip, trip, 0)

    prev_mode = flag_ref[prv]
    overlap = has_cur & has_prev & (prev_mode == MODE_SINGLE)

    def output_matmul():
        return _dot(o_ref[...], w_out_ref[...])

    def output_norm(y):
        xp = xp_ref[...].reshape(rows, D_MODEL)
        out = _layer_norm(ALPHA * xp + y, lng_ref[...], lnb_ref[...])
        y_ref[...] = out.reshape(y_ref.shape)

    def project_and_attend(slot):
        proj_items, record_bound = projection_items(slot)
        attn_items = attention_items(1 - slot)
        held_back = proj_items[len(proj_items) - OUTPUT_OVERLAP_ITEMS[kind]:]
        proj_items = proj_items[:len(proj_items) - OUTPUT_OVERLAP_ITEMS[kind]]
        done = 0
        for i, attend in enumerate(attn_items):
            upto = -(-len(proj_items) * (i + 1) // len(attn_items))
            for item in proj_items[done:upto]:
                item()
            done = upto
            attend()
        y = output_matmul()
        for item in held_back[:-1]:
            item()
        output_norm(y)
        held_back[-1]()
        record_bound()

    for slot in range(2):
        pl.when(overlap & (cur == slot))(functools.partial(project_and_attend, slot))

    @pl.when(jnp.logical_not(overlap))
    def _one_at_a_time():
        for mode in (MODE_SINGLE, MODE_BLOCKS, MODE_PAIRWISE):
            pl.when(has_prev & (prev_mode == mode))(functools.partial(attention_loop, prv, mode))

        @pl.when(has_cur)
        def _project():
            proj_items, record_bound = projection_items(cur)
            for item in proj_items:
                item()
            record_bound()

    @pl.when(has_prev & jnp.logical_not(overlap))
    def _output():
        output_norm(output_matmul())

    @pl.when(has_prev & (prev_t == nt - 1))
    def _write_state():
        so_ref[...] = st_ref[...]


def _weight_operand(w):
    if isinstance(w, tuple):
        arr, layer = w
        nd = arr.ndim - 1
        return arr, pl.BlockSpec((None,) + arr.shape[1:], lambda b, s: (layer,) + (0,) * nd,
                                 pipeline_mode=pl.Buffered(1))
    nd = w.ndim
    return w, pl.BlockSpec(w.shape, lambda b, s: (0,) * nd, pipeline_mode=pl.Buffered(1))


def _mixer_call(kind, layer, x, s0, weights, *, bb, rows_t, chunk):
    B, T, D = x.shape
    heads, dk, dv = (HGRN_HEADS, HGRN_DK, HGRN_DV) if kind == "hgrn" else (GLA_HEADS, GLA_DK, GLA_DV)
    nch = rows_t // chunk
    rows = bb * rows_t
    nt = T // rows_t
    n_blocks = (B // bb) * nt
    grid = (1, n_blocks + 1)
    has_s0 = s0 is not None

    def this_block(_, s):
        blk = jnp.minimum(s, n_blocks - 1)
        return (blk // nt, blk % nt, 0)

    def prev_block(_, s):
        blk = jnp.maximum(s - 1, 0)
        return (blk // nt, blk % nt, 0)

    def prev_group(_, s):
        return (jnp.maximum(s - 1, 0) // nt, 0, 0, 0)

    in_specs = [pl.BlockSpec((bb, rows_t, D), this_block),
                pl.BlockSpec((bb, rows_t, D), prev_block)]
    args = [x, x]
    if has_s0:
        in_specs.append(pl.BlockSpec((bb, heads, dk, dv), prev_group))
        args.append(s0)
    for w in weights:
        arr, spec = _weight_operand(w)
        in_specs.append(spec)
        args.append(arr)

    ncols = 3 * HGRN_FDIM + D_MODEL if kind == "hgrn" else 2 * GLA_KEY_DIM + 2 * GLA_VAL_DIM
    scratch = [
        pltpu.VMEM((2, rows, ncols), F32),
        pltpu.VMEM((2, rows, heads * dk), F32),
        pltpu.VMEM((_chunks_per_trip(chunk, bb * nch), chunk, heads * dk), F32),
        pltpu.VMEM((_chunks_per_trip(chunk, bb * nch), chunk, heads * dk), F32),
        pltpu.VMEM((rows, heads * dv), BF16),
        pltpu.VMEM((bb, heads, dk, dv), F32),
        pltpu.SMEM((2,), jnp.int32),
    ]
    kern = functools.partial(_mixer_kernel, kind=kind, layer=layer, bb=bb, nch=nch, chunk=chunk,
                             nt=nt, has_s0=has_s0)
    return pl.pallas_call(
        kern,
        grid=grid,
        in_specs=in_specs,
        out_specs=[pl.BlockSpec((bb, rows_t, D), prev_block),
                   pl.BlockSpec((bb, heads, dk, dv), prev_group)],
        out_shape=[jax.ShapeDtypeStruct((B, T, D), F32),
                   jax.ShapeDtypeStruct((B, heads, dk, dv), F32)],
        scratch_shapes=scratch,
        compiler_params=pltpu.CompilerParams(
            dimension_semantics=("arbitrary", "arbitrary"),
            vmem_limit_bytes=VMEM_LIMIT_BYTES),
        name=f"{kind}_mixer_{'s' if has_s0 else 'p'}",
    )(*args)


def _ffn_kernel(*refs, bb, rows_t, has_s0):
    it = iter(refs)
    x_ref = next(it)
    c0_ref = next(it) if has_s0 else None
    w_up_ref = next(it)
    cw_ref = next(it)
    cb_ref = next(it)
    w_dn_ref = next(it)
    lng_ref = next(it)
    lnb_ref = next(it)
    y_ref = next(it)
    co_ref = next(it)
    a_ref = next(it)
    h_ref = next(it)
    acc_ref = next(it)

    rows = bb * rows_t
    t = pl.program_id(1)
    nt = pl.num_programs(1)
    pad = SUBLANES

    @pl.when(t == 0)
    def _init_carry():
        if has_s0:
            a_ref[:, pad - 2:pad, :] = c0_ref[...]
        else:
            a_ref[:, pad - 2:pad, :] = jnp.zeros((bb, 2, D_FF), F32)

    x = x_ref[...].reshape(rows, D_MODEL)
    xb = x.astype(BF16)

    def up_proj(n0):
        cs = slice(n0, n0 + FFN_COLS)
        a = _dot(xb, w_up_ref[:, cs])
        a_ref[:, pad:pad + rows_t, cs] = a.reshape(bb, rows_t, FFN_COLS)
        return _dot(xb, w_up_ref[:, D_FF + n0:D_FF + n0 + FFN_COLS])

    def activate(n0, g):
        cs = slice(n0, n0 + FFN_COLS)
        conv = (cb_ref[:, cs]
                + a_ref[:, pad - 2:pad - 2 + rows_t, cs] * cw_ref[0:1, cs]
                + a_ref[:, pad - 1:pad - 1 + rows_t, cs] * cw_ref[1:2, cs]
                + a_ref[:, pad:pad + rows_t, cs] * cw_ref[2:3, cs])
        h_ref[:, cs] = (_gelu_tanh(conv).reshape(rows, FFN_COLS) * g).astype(BF16)

    starts = list(range(0, D_FF, FFN_COLS))
    g_next = up_proj(starts[0])
    for idx, n0 in enumerate(starts):
        g = g_next
        if idx + 1 < len(starts):
            g_next = up_proj(starts[idx + 1])
        activate(n0, g)

    for n0 in range(0, D_MODEL, DOWN_COLS):
        acc_ref[:, n0:n0 + DOWN_COLS] = _dot(h_ref[...], w_dn_ref[:, n0:n0 + DOWN_COLS])

    out = _layer_norm(ALPHA * x + acc_ref[...], lng_ref[...], lnb_ref[...])
    y_ref[...] = out.reshape(y_ref.shape)

    last2 = a_ref[:, pad + rows_t - 2:pad + rows_t, :]
    a_ref[:, pad - 2:pad, :] = last2

    @pl.when(t == nt - 1)
    def _write_carry():
        co_ref[...] = last2


def _ffn_call(x, c0, weights, *, bb, rows_t):
    B, T, D = x.shape
    grid = (B // bb, T // rows_t)
    has_s0 = c0 is not None
    in_specs = [pl.BlockSpec((bb, rows_t, D), lambda b, t: (b, t, 0))]
    args = [x]
    if has_s0:
        in_specs.append(pl.BlockSpec((bb, CONV_WIDTH - 1, D_FF), lambda b, t: (b, 0, 0)))
        args.append(c0)
    for w in weights:
        arr, spec = _weight_operand(w)
        in_specs.append(spec)
        args.append(arr)
    kern = functools.partial(_ffn_kernel, bb=bb, rows_t=rows_t, has_s0=has_s0)
    return pl.pallas_call(
        kern,
        grid=grid,
        in_specs=in_specs,
        out_specs=[pl.BlockSpec((bb, rows_t, D), lambda b, t: (b, t, 0)),
                   pl.BlockSpec((bb, CONV_WIDTH - 1, D_FF), lambda b, t: (b, 0, 0))],
        out_shape=[jax.ShapeDtypeStruct((B, T, D), F32),
                   jax.ShapeDtypeStruct((B, CONV_WIDTH - 1, D_FF), F32)],
        scratch_shapes=[pltpu.VMEM((bb, SUBLANES + rows_t, D_FF), F32),
                        pltpu.VMEM((bb * rows_t, D_FF), BF16),
                        pltpu.VMEM((bb * rows_t, D), F32)],
        compiler_params=pltpu.CompilerParams(
            dimension_semantics=("arbitrary", "arbitrary"),
            vmem_limit_bytes=VMEM_LIMIT_BYTES),
        name=f"conv_ffn_{'s' if has_s0 else 'p'}",
    )(*args)


def _trunk(x, st_hgrn, st_gla, st_conv, layer_weights, *, mixer_bb, ffn_bb, rows_t, chunk):
    new_h, new_g, new_c = [], [], []
    for i in range(DEPTH):
        kind, mix_w, ffn_w = layer_weights[i]
        j = i // 2
        if kind == "hgrn":
            s0 = None if st_hgrn is None else st_hgrn[j]
        else:
            s0 = None if st_gla is None else st_gla[j]
        x, s = _mixer_call(kind, i, x, s0, mix_w, bb=mixer_bb, rows_t=rows_t, chunk=chunk)
        (new_h if kind == "hgrn" else new_g).append(s)
        c0 = None if st_conv is None else st_conv[i]
        x, c = _ffn_call(x, c0, ffn_w, bb=ffn_bb, rows_t=rows_t)
        new_c.append(c)
    return x, jnp.stack(new_h), jnp.stack(new_g), jnp.stack(new_c)


def kernel(x_prompt, x_sample, state_hgrn, state_gla, state_ffn_conv, lb_param, hgrn_w_in, hgrn_w_out, hgrn_norm_g, gla_w_in, gla_w_gk2, gla_b_gk2, gla_w_out, gla_norm_g, ln_mix_g, ln_mix_b, ffn_w_up, ffn_conv_w, ffn_conv_b, ffn_w_down, ln_ffn_g, ln_ffn_b):
    gla_main = 2 * GLA_KEY_DIM + 2 * GLA_VAL_DIM
    hgrn_w_in_b, hgrn_w_out_b = hgrn_w_in.astype(BF16), hgrn_w_out.astype(BF16)
    gla_w_in_b, gla_w_out_b = gla_w_in.astype(BF16), gla_w_out.astype(BF16)
    ffn_w_up_b, ffn_w_down_b = ffn_w_up.astype(BF16), ffn_w_down.astype(BF16)
    layer_weights = []
    for i in range(DEPTH):
        j = i // 2
        if i % 2 == 0:
            mix_w = (lb_param.astype(F32),
                     (hgrn_w_in_b, j),
                     (hgrn_w_out_b, j),
                     hgrn_norm_g[j].reshape(1, HGRN_DV),
                     ln_mix_g[i].reshape(1, D_MODEL),
                     ln_mix_b[i].reshape(1, D_MODEL))
            kind = "hgrn"
        else:
            w_r = jnp.pad(gla_w_in[j][:, gla_main:], ((0, 0), (0, LANES - GLA_GATE_RANK)))
            w_gk2 = jnp.pad(gla_w_gk2[j], ((0, LANES - GLA_GATE_RANK), (0, 0)))
            mix_w = ((gla_w_in_b, j),
                     w_r.astype(BF16),
                     w_gk2.astype(BF16),
                     gla_b_gk2[j].reshape(1, GLA_KEY_DIM),
                     (gla_w_out_b, j),
                     gla_norm_g[j].reshape(1, GLA_DV),
                     ln_mix_g[i].reshape(1, D_MODEL),
                     ln_mix_b[i].reshape(1, D_MODEL))
            kind = "gla"
        ffn_w = ((ffn_w_up_b, i),
                 ffn_conv_w[i],
                 ffn_conv_b[i].reshape(1, D_FF),
                 (ffn_w_down_b, i),
                 ln_ffn_g[i].reshape(1, D_MODEL),
                 ln_ffn_b[i].reshape(1, D_MODEL))
        layer_weights.append((kind, mix_w, ffn_w))

    y_p, h_p, g_p, c_p = _trunk(x_prompt, None, None, None, layer_weights,
                                mixer_bb=1, ffn_bb=1, rows_t=PROMPT_ROWS, chunk=CHUNK)
    dec_b, dec_t = x_sample.shape[0], x_sample.shape[1]
    y_s, h_s, g_s, c_s = _trunk(x_sample, state_hgrn, state_gla, state_ffn_conv, layer_weights,
                                mixer_bb=8, ffn_bb=dec_b, rows_t=dec_t, chunk=dec_t)
    return (y_p, y_s, h_p, h_s, g_p, g_s, c_p, c_s)
```

```python
import functools
import math

import jax
import jax.numpy as jnp
from jax import lax
from jax.experimental import pallas as pl
from jax.experimental.pallas import tpu as pltpu

F32 = jnp.float32
BF16 = jnp.bfloat16

D_MODEL = 1024
DEPTH = 2
CHUNK = 64
HGRN_HEADS = 8
HGRN_DK = 128
HGRN_DV = 128
HGRN_FDIM = HGRN_HEADS * HGRN_DK
GLA_HEADS = 4
GLA_KEY_DIM = 512
GLA_VAL_DIM = 1024
GLA_DK = 128
GLA_DV = 256
GLA_GATE_RANK = 16
GLA_GATE_NORMALIZER = 16.0
D_FF = 2816
CONV_WIDTH = 3
ALPHA = (2.0 * DEPTH) ** 0.25
LN_EPS = 1e-5
RMS_EPS = 1e-6
LOG2E = 1.4426950408889634

SUBLANES = 8
LANES = 128
VMEM_LIMIT_BYTES = 56 * 1024 * 1024
PROMPT_ROWS = 512
FFN_COLS = 256
DOWN_COLS = 512
PROJ_COLS = {"hgrn": 512, "gla": 256}
MAX_FACTOR_LOG2 = 100.0
FAST_BLOCK_ROWS = 2 * SUBLANES
MODE_SINGLE, MODE_BLOCKS, MODE_PAIRWISE = 2, 1, 0
TRIP_ROWS = 128


def _chunks_per_trip(chunk, n_chunks):
    return min(n_chunks, max(TRIP_ROWS // chunk, 1))


OUTPUT_OVERLAP_ITEMS = {"hgrn": 2, "gla": 4}


def _dot(a, b):
    return jnp.dot(a, b, preferred_element_type=F32)


def _dot_nt(a, b):
    return lax.dot_general(a, b, (((1,), (1,)), ((), ())), preferred_element_type=F32)


def _dot_tn(a, b):
    return lax.dot_general(a, b, (((0,), (0,)), ((), ())), preferred_element_type=F32)


def _sigmoid(x):
    return 1.0 / (1.0 + jnp.exp(-x))


def _silu(x):
    return x * _sigmoid(x)


def _gelu_tanh(x):
    c = math.sqrt(2.0 / math.pi)
    k = -2.0 * c * LOG2E
    z = x * ((k * 0.044715) * (x * x) + k)
    return x / (1.0 + jnp.exp2(z))


def _log2_sigmoid(x):
    return jnp.minimum(x, 0.0) * LOG2E - jnp.log2(1.0 + jnp.exp2(jnp.abs(x) * (-LOG2E)))


def _layer_norm(x, g, b):
    mu = jnp.mean(x, axis=-1, keepdims=True)
    xc = x - mu
    var = jnp.mean(xc * xc, axis=-1, keepdims=True)
    return xc * lax.rsqrt(var + LN_EPS) * g + b


def _cumsum_rows(g):
    c = g.shape[0]
    r = lax.broadcasted_iota(jnp.int32, (c, c), 0)
    s = lax.broadcasted_iota(jnp.int32, (c, c), 1)
    tri = (s <= r).astype(BF16)
    hi = g.astype(BF16)
    lo = (g - hi.astype(F32)).astype(BF16)
    return _dot(tri, hi) + _dot(tri, lo)


def _mode_limits(chunk):
    half_windows = max(chunk // 2 // SUBLANES, 1)
    block_windows = min(FAST_BLOCK_ROWS, chunk) // SUBLANES
    return MAX_FACTOR_LOG2 / half_windows, MAX_FACTOR_LOG2 / block_windows


def _head_attention(q_view, k_view, b_view, bc_ref, kc_ref, r0, h, dk, v, chunk, mode):
    exact_diag = mode == MODE_PAIRWISE
    if mode == MODE_PAIRWISE:
        R = SUBLANES
    elif mode == MODE_BLOCKS:
        R = min(FAST_BLOCK_ROWS, chunk)
    else:
        R = chunk
    ref_in_block = R // 2 - 1 if mode == MODE_SINGLE else R - 1
    nb = chunk // R
    static_rows = isinstance(r0, int)

    def cols(view):
        return slice(view[1] + h * dk, view[1] + (h + 1) * dk)

    def blk(view, i):
        if static_rows:
            return view[0][r0 + R * i:r0 + R * (i + 1), cols(view)]
        return view[0][pl.ds(pl.multiple_of(r0 + R * i, R), R), cols(view)]

    def row1(view, copy_ref, t):
        if static_rows:
            return view[0][r0 + t:r0 + t + 1, cols(view)]
        return copy_ref[t:t + 1, h * dk:(h + 1) * dk]

    def row(view, copy_ref, t):
        return jnp.broadcast_to(row1(view, copy_ref, t), (R, dk))

    def full(view):
        if static_rows:
            return view[0][r0:r0 + chunk, cols(view)]
        return view[0][pl.ds(r0, chunk), cols(view)]

    def b_end(j):
        return row(b_view, bc_ref, R * j + ref_in_block)

    b_last = row1(b_view, bc_ref, chunk - 1)
    kpp_f32 = [blk(k_view, j) * jnp.exp2(b_end(j) - blk(b_view, j)) for j in range(nb)]
    kpp = jnp.concatenate(kpp_f32, axis=0).astype(BF16)
    kd = jnp.concatenate(
        [kpp_f32[j] * jnp.exp2(b_last - row1(b_view, bc_ref, R * j + ref_in_block))
         for j in range(nb)], axis=0).astype(BF16)

    col = lax.broadcasted_iota(jnp.int32, (R, chunk), 1)
    rw = lax.broadcasted_iota(jnp.int32, (R, chunk), 0)
    colblk = col // R

    lq, pair = [], {}
    for i in range(nb):
        qi, bi = blk(q_view, i), blk(b_view, i)
        for j in range(i if exact_diag else i + 1):
            pair[(i, j)] = len(lq)
            lq.append(qi * jnp.exp2(bi - b_end(j)))
    if lq:
        cross = _dot_nt(jnp.concatenate(lq, axis=0).astype(BF16), kpp)

    def cross_blk(i, j):
        p = pair[(i, j)]
        return cross[R * p:R * (p + 1), :]

    vb = v.astype(BF16)
    qe = (full(q_view) * jnp.exp2(full(b_view))).astype(BF16)
    kv = _dot_tn(kd, vb)
    dv = v.shape[1]
    decay = jnp.broadcast_to(jnp.exp2(b_last), (dk, dk)).T
    if dv > dk:
        decay = jnp.concatenate([decay] * (dv // dk), axis=1)

    def finish(st):
        a_rows = []
        for i in range(nb):
            if exact_diag:
                acc = jnp.zeros((R, chunk), F32)
                for j in range(i):
                    acc = jnp.where(colblk == j, cross_blk(i, j), acc)
                qi, bi = blk(q_view, i), blk(b_view, i)
                for s in range(R):
                    t = R * i + s
                    e = jnp.exp2(jnp.minimum(bi - row(b_view, bc_ref, t), 0.0))
                    cs = jnp.sum(qi * e * row(k_view, kc_ref, t), axis=-1, keepdims=True)
                    acc = jnp.where(col == t, cs, acc)
            else:
                acc = cross_blk(i, 0)
                for j in range(1, i + 1):
                    acc = jnp.where(colblk == j, cross_blk(i, j), acc)
            a_rows.append(jnp.where(col <= R * i + rw, acc, 0.0))
        a = jnp.concatenate(a_rows, axis=0).astype(BF16)
        o = _dot(jnp.concatenate([qe, a], axis=1),
                 jnp.concatenate([st.astype(BF16), vb], axis=0))
        st_new = decay * st + kv
        return o, st_new

    return finish


def _rms_gate(o, norm_g, gate_act):
    ms = jnp.mean(o * o, axis=-1, keepdims=True)
    return o * lax.rsqrt(ms + RMS_EPS) * norm_g * gate_act


def _mixer_kernel(*refs, kind, layer, bb, nch, chunk, nt, has_s0):
    it = iter(refs)
    x_ref = next(it)
    xp_ref = next(it)
    s0_ref = next(it) if has_s0 else None
    if kind == "hgrn":
        lb_ref = next(it)
        w_in_ref = next(it)
    else:
        w_in_ref = next(it)
        w_r_ref = next(it)
        w_gk2_ref = next(it)
        b_gk2_ref = next(it)
    w_out_ref = next(it)
    ng_ref = next(it)
    lng_ref = next(it)
    lnb_ref = next(it)
    y_ref = next(it)
    so_ref = next(it)
    proj2_ref = next(it)
    aux2_ref = next(it)
    bc_ref = next(it)
    kc_ref = next(it)
    o_ref = next(it)
    st_ref = next(it)
    flag_ref = next(it)

    heads, dk, dv = (HGRN_HEADS, HGRN_DK, HGRN_DV) if kind == "hgrn" else (GLA_HEADS, GLA_DK, GLA_DV)
    kdim = heads * dk
    rows = bb * nch * chunk
    n_chunks = bb * nch
    per_item = _chunks_per_trip(chunk, n_chunks)
    nb = chunk // SUBLANES
    ncols = proj2_ref.shape[2]
    s = pl.program_id(1)
    ns = pl.num_programs(1)
    cur = s % 2
    prv = 1 - cur
    has_cur = s < ns - 1
    has_prev = s > 0
    prev_t = (s - 1) % nt

    if kind == "hgrn":
        q_off, f_off, v_off, g_off = 0, kdim, 2 * kdim, 3 * kdim
    else:
        q_off, k_off, v_off, g_off = 0, kdim, 2 * kdim, 2 * kdim + heads * dv

    def views(slot):
        proj, aux = proj2_ref.at[slot], aux2_ref.at[slot]
        if kind == "hgrn":
            return proj, aux, (proj, q_off), (aux, 0), (proj, f_off)
        return proj, aux, (proj, q_off), (proj, k_off), (aux, 0)

    @pl.when(s == 0)
    def _init_flags():
        flag_ref[0] = 0
        flag_ref[1] = 0

    @pl.when(has_prev & (prev_t == 0))
    def _init_state():
        if has_s0:
            st_ref[...] = s0_ref[...]
        else:
            st_ref[...] = jnp.zeros(st_ref.shape, F32)

    ng = ng_ref[...]

    def projection_items(slot):
        proj, aux, _, _, _ = views(slot)
        xb = x_ref[...].reshape(rows, D_MODEL).astype(BF16)
        decay_max = []

        def cumsum_in_place(ref, cs):
            m = jnp.zeros((SUBLANES, cs.stop - cs.start), F32)
            for c in range(n_chunks):
                rs = slice(c * chunk, (c + 1) * chunk)
                b = _cumsum_rows(ref[rs, cs])
                ref[rs, cs] = b
                m = jnp.maximum(m, -b[0:SUBLANES])
                for j in range(1, nb):
                    m = jnp.maximum(m, b[SUBLANES * (j - 1):SUBLANES * j] - b[SUBLANES * j:SUBLANES * (j + 1)])
            decay_max.append(jnp.max(m))

        if kind == "hgrn":
            p = lb_ref[...]
            pe = jnp.exp(p - jnp.max(p, axis=0, keepdims=True))
            lb = jnp.sum(pe[:layer + 1], axis=0, keepdims=True) / jnp.sum(pe, axis=0, keepdims=True)

        def col_block(n0):
            cs = slice(n0, n0 + PROJ_COLS[kind])
            blk = _dot(xb, w_in_ref[:, cs])
            if n0 >= g_off:
                proj[:, cs] = _silu(blk)
            elif n0 >= v_off:
                proj[:, cs] = blk
            elif kind == "hgrn" and n0 >= f_off:
                fs = slice(n0 - f_off, n0 - f_off + PROJ_COLS[kind])
                f = lb[:, fs] + (1.0 - lb[:, fs]) * _sigmoid(blk)
                aux[:, fs] = 1.0 - f
                proj[:, cs] = jnp.log2(f)
                cumsum_in_place(proj, cs)
            elif kind == "hgrn":
                proj[:, cs] = _silu(blk)
            elif n0 >= k_off:
                proj[:, cs] = blk
            else:
                proj[:, cs] = blk * (dk ** -0.5)

        def gla_gate():
            r = _dot(xb, w_r_ref[...]).astype(BF16)
            z = _dot(r, w_gk2_ref[...]) + b_gk2_ref[...]
            aux[...] = _log2_sigmoid(z) * (1.0 / GLA_GATE_NORMALIZER)
            cumsum_in_place(aux, slice(0, kdim))

        def record_bound():
            worst = functools.reduce(jnp.maximum, decay_max)
            single_max, blocks_max = _mode_limits(chunk)
            flag_ref[slot] = jnp.where(
                worst <= single_max, MODE_SINGLE,
                jnp.where(worst <= blocks_max, MODE_BLOCKS, MODE_PAIRWISE)).astype(jnp.int32)

        items = [functools.partial(col_block, n0) for n0 in range(0, ncols, PROJ_COLS[kind])]
        if kind == "gla":
            items.append(gla_gate)
        return items, record_bound

    def finish_chunk(proj, c_rows, b_idx, finishers):
        for h, finish in enumerate(finishers):
            o, st_new = finish(st_ref[b_idx, h])
            st_ref[b_idx, h] = st_new
            gate = proj[c_rows, g_off + h * dv:g_off + (h + 1) * dv]
            o_ref[c_rows, h * dv:(h + 1) * dv] = _rms_gate(o, ng, gate).astype(BF16)

    def attention_items(slot):
        proj, _, q_view, k_view, b_view = views(slot)
        items, staged = [], {}

        def issue(chunks):
            for c in chunks:
                r0 = c * chunk
                staged[c] = [_head_attention(q_view, k_view, b_view, None, None, r0, h, dk,
                                             proj[r0:r0 + chunk, v_off + h * dv:v_off + (h + 1) * dv],
                                             chunk, MODE_SINGLE) for h in range(heads)]

        def consume(chunks):
            for c in chunks:
                finish_chunk(proj, slice(c * chunk, (c + 1) * chunk), c // nch, staged.pop(c))

        for c0 in range(0, n_chunks, per_item):
            chunks = list(range(c0, c0 + per_item))
            items += [functools.partial(issue, chunks), functools.partial(consume, chunks)]
        return items

    def attention_loop(slot, mode):
        exact_diag = mode == MODE_PAIRWISE
        proj, _, q_view, k_view, b_view = views(slot)

        def first_stage(ci, copy):
            r0 = pl.multiple_of(ci * chunk, chunk)
            rs = pl.ds(r0, chunk)
            bc, kc = bc_ref.at[copy], kc_ref.at[copy]
            bc[...] = b_view[0][rs, b_view[1]:b_view[1] + kdim]
            if exact_diag:
                kc[...] = k_view[0][rs, k_view[1]:k_view[1] + kdim]
            return [_head_attention(q_view, k_view, b_view, bc, kc, r0, h, dk,
                                    proj[rs, v_off + h * dv:v_off + (h + 1) * dv],
                                    chunk, mode) for h in range(heads)]

        per_trip = 1 if exact_diag else per_item

        def trip(ti, carry):
            staged = [first_stage(ti * per_trip + j, j) for j in range(per_trip)]
            for j in range(per_trip):
                ci = ti * per_trip + j
                finish_chunk(proj, pl.ds(pl.multiple_of(ci * chunk, chunk), chunk), ci // nch, staged[j])
            return carry
        lax.fori_loop(0, n_chunks // per_trip, trip, 0)

    prev_mode = flag_ref[prv]
    overlap = has_cur & has_prev & (prev_mode == MODE_SINGLE)

    def output_matmul():
        return _dot(o_ref[...], w_out_ref[...])

    def output_norm(y):
        xp = xp_ref[...].reshape(rows, D_MODEL)
        out = _layer_norm(ALPHA * xp + y, lng_ref[...], lnb_ref[...])
        y_ref[...] = out.reshape(y_ref.shape)

    def project_and_attend(slot):
        proj_items, record_bound = projection_items(slot)
        attn_items = attention_items(1 - slot)
        held_back = proj_items[len(proj_items) - OUTPUT_OVERLAP_ITEMS[kind]:]
        proj_items = proj_items[:len(proj_items) - OUTPUT_OVERLAP_ITEMS[kind]]
        done = 0
        for i, attend in enumerate(attn_items):
            upto = -(-len(proj_items) * (i + 1) // len(attn_items))
            for item in proj_items[done:upto]:
                item()
            done = upto
            attend()
        y = output_matmul()
        for item in held_back[:-1]:
            item()
        output_norm(y)
        held_back[-1]()
        record_bound()

    for slot in range(2):
        pl.when(overlap & (cur == slot))(functools.partial(project_and_attend, slot))

    @pl.when(jnp.logical_not(overlap))
    def _one_at_a_time():
        for mode in (MODE_SINGLE, MODE_BLOCKS, MODE_PAIRWISE):
            pl.when(has_prev & (prev_mode == mode))(functools.partial(attention_loop, prv, mode))

        @pl.when(has_cur)
        def _project():
            proj_items, record_bound = projection_items(cur)
            for item in proj_items:
                item()
            record_bound()

    @pl.when(has_prev & jnp.logical_not(overlap))
    def _output():
        output_norm(output_matmul())

    @pl.when(has_prev & (prev_t == nt - 1))
    def _write_state():
        so_ref[...] = st_ref[...]


def _weight_operand(w):
    if isinstance(w, tuple):
        arr, layer = w
        nd = arr.ndim - 1
        return arr, pl.BlockSpec((None,) + arr.shape[1:], lambda b, s: (layer,) + (0,) * nd,
                                 pipeline_mode=pl.Buffered(1))
    nd = w.ndim
    return w, pl.BlockSpec(w.shape, lambda b, s: (0,) * nd, pipeline_mode=pl.Buffered(1))


def _mixer_call(kind, layer, x, s0, weights, *, bb, rows_t, chunk):
    B, T, D = x.shape
    heads, dk, dv = (HGRN_HEADS, HGRN_DK, HGRN_DV) if kind == "hgrn" else (GLA_HEADS, GLA_DK, GLA_DV)
    nch = rows_t // chunk
    rows = bb * rows_t
    nt = T // rows_t
    n_blocks = (B // bb) * nt
    grid = (1, n_blocks + 1)
    has_s0 = s0 is not None

    def this_block(_, s):
        blk = jnp.minimum(s, n_blocks - 1)
        return (blk // nt, blk % nt, 0)

    def prev_block(_, s):
        blk = jnp.maximum(s - 1, 0)
        return (blk // nt, blk % nt, 0)

    def prev_group(_, s):
        return (jnp.maximum(s - 1, 0) // nt, 0, 0, 0)

    in_specs = [pl.BlockSpec((bb, rows_t, D), this_block),
                pl.BlockSpec((bb, rows_t, D), prev_block)]
    args = [x, x]
    if has_s0:
        in_specs.append(pl.BlockSpec((bb, heads, dk, dv), prev_group))
        args.append(s0)
    for w in weights:
        arr, spec = _weight_operand(w)
        in_specs.append(spec)
        args.append(arr)

    ncols = 3 * HGRN_FDIM + D_MODEL if kind == "hgrn" else 2 * GLA_KEY_DIM + 2 * GLA_VAL_DIM
    scratch = [
        pltpu.VMEM((2, rows, ncols), F32),
        pltpu.VMEM((2, rows, heads * dk), F32),
        pltpu.VMEM((_chunks_per_trip(chunk, bb * nch), chunk, heads * dk), F32),
        pltpu.VMEM((_chunks_per_trip(chunk, bb * nch), chunk, heads * dk), F32),
        pltpu.VMEM((rows, heads * dv), BF16),
        pltpu.VMEM((bb, heads, dk, dv), F32),
        pltpu.SMEM((2,), jnp.int32),
    ]
    kern = functools.partial(_mixer_kernel, kind=kind, layer=layer, bb=bb, nch=nch, chunk=chunk,
                             nt=nt, has_s0=has_s0)
    return pl.pallas_call(
        kern,
        grid=grid,
        in_specs=in_specs,
        out_specs=[pl.BlockSpec((bb, rows_t, D), prev_block),
                   pl.BlockSpec((bb, heads, dk, dv), prev_group)],
        out_shape=[jax.ShapeDtypeStruct((B, T, D), F32),
                   jax.ShapeDtypeStruct((B, heads, dk, dv), F32)],
        scratch_shapes=scratch,
        compiler_params=pltpu.CompilerParams(
            dimension_semantics=("arbitrary", "arbitrary"),
            vmem_limit_bytes=VMEM_LIMIT_BYTES),
        name=f"{kind}_mixer_{'s' if has_s0 else 'p'}",
    )(*args)


def _ffn_kernel(*refs, bb, rows_t, has_s0):
    it = iter(refs)
    x_ref = next(it)
    c0_ref = next(it) if has_s0 else None
    w_up_ref = next(it)
    cw_ref = next(it)
    cb_ref = next(it)
    w_dn_ref = next(it)
    lng_ref = next(it)
    lnb_ref = next(it)
    y_ref = next(it)
    co_ref = next(it)
    a_ref = next(it)
    h_ref = next(it)
    acc_ref = next(it)

    rows = bb * rows_t
    t = pl.program_id(1)
    nt = pl.num_programs(1)
    pad = SUBLANES

    @pl.when(t == 0)
    def _init_carry():
        if has_s0:
            a_ref[:, pad - 2:pad, :] = c0_ref[...]
        else:
            a_ref[:, pad - 2:pad, :] = jnp.zeros((bb, 2, D_FF), F32)

    x = x_ref[...].reshape(rows, D_MODEL)
    xb = x.astype(BF16)

    def up_proj(n0):
        cs = slice(n0, n0 + FFN_COLS)
        a = _dot(xb, w_up_ref[:, cs])
        a_ref[:, pad:pad + rows_t, cs] = a.reshape(bb, rows_t, FFN_COLS)
        return _dot(xb, w_up_ref[:, D_FF + n0:D_FF + n0 + FFN_COLS])

    def activate(n0, g):
        cs = slice(n0, n0 + FFN_COLS)
        conv = (cb_ref[:, cs]
                + a_ref[:, pad - 2:pad - 2 + rows_t, cs] * cw_ref[0:1, cs]
                + a_ref[:, pad - 1:pad - 1 + rows_t, cs] * cw_ref[1:2, cs]
                + a_ref[:, pad:pad + rows_t, cs] * cw_ref[2:3, cs])
        h_ref[:, cs] = (_gelu_tanh(conv).reshape(rows, FFN_COLS) * g).astype(BF16)

    starts = list(range(0, D_FF, FFN_COLS))
    g_next = up_proj(starts[0])
    for idx, n0 in enumerate(starts):
        g = g_next
        if idx + 1 < len(starts):
            g_next = up_proj(starts[idx + 1])
        activate(n0, g)

    for n0 in range(0, D_MODEL, DOWN_COLS):
        acc_ref[:, n0:n0 + DOWN_COLS] = _dot(h_ref[...], w_dn_ref[:, n0:n0 + DOWN_COLS])

    out = _layer_norm(ALPHA * x + acc_ref[...], lng_ref[...], lnb_ref[...])
    y_ref[...] = out.reshape(y_ref.shape)

    last2 = a_ref[:, pad + rows_t - 2:pad + rows_t, :]
    a_ref[:, pad - 2:pad, :] = last2

    @pl.when(t == nt - 1)
    def _write_carry():
        co_ref[...] = last2


def _ffn_call(x, c0, weights, *, bb, rows_t):
    B, T, D = x.shape
    grid = (B // bb, T // rows_t)
    has_s0 = c0 is not None
    in_specs = [pl.BlockSpec((bb, rows_t, D), lambda b, t: (b, t, 0))]
    args = [x]
    if has_s0:
        in_specs.append(pl.BlockSpec((bb, CONV_WIDTH - 1, D_FF), lambda b, t: (b, 0, 0)))
        args.append(c0)
    for w in weights:
        arr, spec = _weight_operand(w)
        in_specs.append(spec)
        args.append(arr)
    kern = functools.partial(_ffn_kernel, bb=bb, rows_t=rows_t, has_s0=has_s0)
    return pl.pallas_call(
        kern,
        grid=grid,
        in_specs=in_specs,
        out_specs=[pl.BlockSpec((bb, rows_t, D), lambda b, t: (b, t, 0)),
                   pl.BlockSpec((bb, CONV_WIDTH - 1, D_FF), lambda b, t: (b, 0, 0))],
        out_shape=[jax.ShapeDtypeStruct((B, T, D), F32),
                   jax.ShapeDtypeStruct((B, CONV_WIDTH - 1, D_FF), F32)],
        scratch_shapes=[pltpu.VMEM((bb, SUBLANES + rows_t, D_FF), F32),
                        pltpu.VMEM((bb * rows_t, D_FF), BF16),
                        pltpu.VMEM((bb * rows_t, D), F32)],
        compiler_params=pltpu.CompilerParams(
            dimension_semantics=("arbitrary", "arbitrary"),
            vmem_limit_bytes=VMEM_LIMIT_BYTES),
        name=f"conv_ffn_{'s' if has_s0 else 'p'}",
    )(*args)


def _trunk(x, st_hgrn, st_gla, st_conv, layer_weights, *, mixer_bb, ffn_bb, rows_t, chunk):
    new_h, new_g, new_c = [], [], []
    for i in range(DEPTH):
        kind, mix_w, ffn_w = layer_weights[i]
        j = i // 2
        if kind == "hgrn":
            s0 = None if st_hgrn is None else st_hgrn[j]
        else:
            s0 = None if st_gla is None else st_gla[j]
        x, s = _mixer_call(kind, i, x, s0, mix_w, bb=mixer_bb, rows_t=rows_t, chunk=chunk)
        (new_h if kind == "hgrn" else new_g).append(s)
        c0 = None if st_conv is None else st_conv[i]
        x, c = _ffn_call(x, c0, ffn_w, bb=ffn_bb, rows_t=rows_t)
        new_c.append(c)
    return x, jnp.stack(new_h), jnp.stack(new_g), jnp.stack(new_c)


def kernel(x_prompt, x_sample, state_hgrn, state_gla, state_ffn_conv, lb_param, hgrn_w_in, hgrn_w_out, hgrn_norm_g, gla_w_in, gla_w_gk2, gla_b_gk2, gla_w_out, gla_norm_g, ln_mix_g, ln_mix_b, ffn_w_up, ffn_conv_w, ffn_conv_b, ffn_w_down, ln_ffn_g, ln_ffn_b):
    gla_main = 2 * GLA_KEY_DIM + 2 * GLA_VAL_DIM
    hgrn_w_in_b, hgrn_w_out_b = hgrn_w_in.astype(BF16), hgrn_w_out.astype(BF16)
    gla_w_in_b, gla_w_out_b = gla_w_in.astype(BF16), gla_w_out.astype(BF16)
    ffn_w_up_b, ffn_w_down_b = ffn_w_up.astype(BF16), ffn_w_down.astype(BF16)
    layer_weights = []
    for i in range(DEPTH):
        j = i // 2
        if i % 2 == 0:
            mix_w = (lb_param.astype(F32),
                     (hgrn_w_in_b, j),
                     (hgrn_w_out_b, j),
                     hgrn_norm_g[j].reshape(1, HGRN_DV),
                     ln_mix_g[i].reshape(1, D_MODEL),
                     ln_mix_b[i].reshape(1, D_MODEL))
            kind = "hgrn"
        else:
            w_r = jnp.pad(gla_w_in[j][:, gla_main:], ((0, 0), (0, LANES - GLA_GATE_RANK)))
            w_gk2 = jnp.pad(gla_w_gk2[j], ((0, LANES - GLA_GATE_RANK), (0, 0)))
            mix_w = ((gla_w_in_b, j),
                     w_r.astype(BF16),
                     w_gk2.astype(BF16),
                     gla_b_gk2[j].reshape(1, GLA_KEY_DIM),
                     (gla_w_out_b, j),
                     gla_norm_g[j].reshape(1, GLA_DV),
                     ln_mix_g[i].reshape(1, D_MODEL),
                     ln_mix_b[i].reshape(1, D_MODEL))
            kind = "gla"
        ffn_w = ((ffn_w_up_b, i),
                 ffn_conv_w[i],
                 ffn_conv_b[i].reshape(1, D_FF),
                 (ffn_w_down_b, i),
                 ln_ffn_g[i].reshape(1, D_MODEL),
                 ln_ffn_b[i].reshape(1, D_MODEL))
        layer_weights.append((kind, mix_w, ffn_w))

    y_p, h_p, g_p, c_p = _trunk(x_prompt, None, None, None, layer_weights,
                                mixer_bb=1, ffn_bb=1, rows_t=PROMPT_ROWS, chunk=CHUNK)
    dec_b, dec_t = x_sample.shape[0], x_sample.shape[1]
    y_s, h_s, g_s, c_s = _trunk(x_sample, state_hgrn, state_gla, state_ffn_conv, layer_weights,
                                mixer_bb=8, ffn_bb=dec_b, rows_t=dec_t, chunk=dec_t)
    return (y_p, y_s, h_p, h_s, g_p, g_s, c_p, c_s)
```
